```python
import math
import jax, jax.numpy as jnp
from jax import lax
import numpy as np

D_MODEL = 2048
BATCH = 4
SEQ = 2048
DEPTH = 4
DEC_BATCH = 128
DEC_SEQ = 1
PAST_LEN = 8192
PAGE_SIZE = 128

N_PAGES = PAST_LEN // PAGE_SIZE

N_GROUPS = 4
D_MIX = D_MODEL
GROUP_WIDTH = D_MIX // N_GROUPS
D_FF = ((8 * D_MODEL // 3 + 127) // 128) * 128
EPS = 1e-6
Q_BLOCK = 128

FOX_HEADS = 4
FOX_HEAD_DIM = GROUP_WIDTH // FOX_HEADS
FOX_KV_HEADS = 1
FOX_GROUP = FOX_HEADS // FOX_KV_HEADS
FOX_SCALE = FOX_HEAD_DIM ** -0.5
FOX_GATE_BIAS = 5.0

CONV_CH = GROUP_WIDTH
CONV_WIDTH = 3

MLA_HEADS = 4
MLA_NOPE = 128
MLA_ROPE = 64
MLA_V = GROUP_WIDTH // MLA_HEADS
MLA_Q_RANK = 512
MLA_KV_RANK = 256
MLA_SCALE = (MLA_NOPE + MLA_ROPE) ** -0.5
ROPE_THETA = 10000.0

S5_CH = GROUP_WIDTH
S5_GROUP_SIZE = 16
S5_GROUPS = S5_CH // S5_GROUP_SIZE
S5_STATE = 64
S5_DT_MIN = 0.001
S5_DT_MAX = 0.1

SPLIT_SIZES = (
    FOX_HEADS * FOX_HEAD_DIM,
    FOX_KV_HEADS * FOX_HEAD_DIM,
    FOX_KV_HEADS * FOX_HEAD_DIM,
    FOX_HEADS,
    CONV_CH,
    CONV_CH,
    CONV_CH,
    MLA_Q_RANK,
    MLA_KV_RANK,
    MLA_ROPE,
    S5_CH,
)
D_IN = sum(SPLIT_SIZES)

STATE_KEYS = ("fox_k_p", "fox_v_p", "fox_logf_p", "mla_ckv_p", "mla_krope_p", "conv_p", "s5_re_p", "s5_im_p",
              "fox_k_s", "fox_v_s", "fox_logf_s", "mla_ckv_s", "mla_krope_s", "conv_s", "s5_re_s", "s5_im_s")

kernel_name = "hybrid_fox_conv_mla_s5_macaron_step"


def rmsnorm(x, g):
    xf = x.astype(jnp.float32)
    y = xf * lax.rsqrt(jnp.mean(xf * xf, axis=-1, keepdims=True) + EPS)
    return (y * g.astype(jnp.float32)).astype(x.dtype)


def swiglu_ffn(x, g, w_gate, w_up, w_down):
    h = rmsnorm(x, g)
    return (jax.nn.silu(h @ w_gate) * (h @ w_up)) @ w_down


def rope(x, pos):
    half = x.shape[-1] // 2
    inv_freq = ROPE_THETA ** (-jnp.arange(half, dtype=jnp.float32) / half)
    ang = pos.astype(jnp.float32)[:, None] * inv_freq
    ang = ang.reshape((1, ang.shape[0]) + (1,) * (x.ndim - 3) + (half,))
    cos, sin = jnp.cos(ang), jnp.sin(ang)
    xf = x.astype(jnp.float32)
    x1, x2 = xf[..., :half], xf[..., half:]
    return jnp.concatenate([x1 * cos - x2 * sin, x1 * sin + x2 * cos], axis=-1).astype(x.dtype)


def causal_mask(q_pos, k_pos):
    return k_pos[None, :] <= q_pos[:, None]


def query_blocks(fn, q_arrays):
    b, n = q_arrays[0].shape[:2]

    def body(i):
        start = i * Q_BLOCK
        blocks = [lax.dynamic_slice_in_dim(a, start, Q_BLOCK, axis=1) for a in q_arrays]
        return fn(start + jnp.arange(Q_BLOCK), *blocks)

    out = jnp.moveaxis(lax.map(body, jnp.arange(n // Q_BLOCK)), 0, 1)
    return out.reshape((b, n) + out.shape[3:])


def joint_softmax(s_past, s_new):
    p = jax.nn.softmax(jnp.concatenate([s_past, s_new], axis=-1), axis=-1)
    n_past = s_past.shape[-1]
    return p[..., :n_past], p[..., n_past:]


def paged_rows(cache, l, page_table):
    g = cache[l, page_table]
    return g.reshape((g.shape[0], g.shape[1] * g.shape[2]) + g.shape[3:])


def heads_last(c):
    return c.reshape(c.shape[0], c.shape[1], FOX_KV_HEADS, FOX_GROUP).transpose(0, 2, 3, 1)


def fox_prompt(q, k, v, logf):
    b, s = q.shape[:2]
    cum = jnp.cumsum(logf, axis=1)
    cum_k = heads_last(cum)[:, :, :, None, :]
    qg = q.reshape(b, s, FOX_KV_HEADS, FOX_GROUP, FOX_HEAD_DIM)
    k_pos = jnp.arange(s)

    def block(q_pos, qb, cqb):
        sc = jnp.einsum('bqkgd,bskd->bkgqs', qb, k).astype(jnp.float32) * FOX_SCALE
        sc = sc + heads_last(cqb)[..., None] - cum_k
        sc = jnp.where(causal_mask(q_pos, k_pos), sc, -jnp.inf)
        p = jax.nn.softmax(sc, axis=-1).astype(v.dtype)
        o = jnp.einsum('bkgqs,bskd->bqkgd', p, v)
        return o.reshape(b, Q_BLOCK, FOX_HEADS * FOX_HEAD_DIM)

    return query_blocks(block, [qg, cum])


def fox_sample(q, k, v, logf, k_past, v_past, logf_past):
    b, t = q.shape[:2]
    cum_past = jnp.cumsum(logf_past.astype(jnp.float32), axis=1)
    cum_new = cum_past[:, -1:] + jnp.cumsum(logf, axis=1)
    cq = heads_last(cum_new)[..., None]
    qg = q.reshape(b, t, FOX_KV_HEADS, FOX_GROUP, FOX_HEAD_DIM)
    s_past = (jnp.einsum('btkgd,bskd->bkgts', qg, k_past).astype(jnp.float32) * FOX_SCALE
              + cq - heads_last(cum_past)[:, :, :, None, :])
    s_new = (jnp.einsum('btkgd,bskd->bkgts', qg, k).astype(jnp.float32) * FOX_SCALE
             + cq - heads_last(cum_new)[:, :, :, None, :])
    pos = jnp.arange(t)
    s_new = jnp.where(causal_mask(pos, pos), s_new, -jnp.inf)
    p_past, p_new = joint_softmax(s_past, s_new)
    o = (jnp.einsum('bkgts,bskd->btkgd', p_past.astype(v.dtype), v_past)
         + jnp.einsum('bkgts,bskd->btkgd', p_new.astype(v.dtype), v))
    return o.reshape(b, t, FOX_HEADS * FOX_HEAD_DIM)


def short_conv_mixer(v_in, gate_b, gate_c, prefix, conv_w):
    xc = gate_c * v_in
    xp = jnp.concatenate([prefix.astype(xc.dtype), xc], axis=1)
    y = lax.conv_general_dilated(xp, conv_w[:, None, :].astype(xc.dtype), window_strides=(1,), padding='VALID',
                                 dimension_numbers=('NWC', 'WIO', 'NWC'), feature_group_count=CONV_CH)
    return gate_b * y, xp[:, -(CONV_WIDTH - 1):]


def mla_project(c_q, c_kv, k_rope_raw, pos, g_q, w_uq, g_kv, w_uk):
    b, t = c_q.shape[:2]
    q = (rmsnorm(c_q, g_q) @ w_uq).reshape(b, t, MLA_HEADS, MLA_NOPE + MLA_ROPE)
    q_nope, q_rope = q[..., :MLA_NOPE], rope(q[..., MLA_NOPE:], pos)
    q_lat = jnp.einsum('bthn,rhn->bthr', q_nope, w_uk)
    ckv = rmsnorm(c_kv, g_kv)
    k_rope = rope(k_rope_raw, pos)
    return q_lat, q_rope, ckv, k_rope


def mla_scores(q_lat, q_rope, ckv, k_rope):
    s = jnp.einsum('bthr,bsr->bhts', q_lat, ckv) + jnp.einsum('bthe,bse->bhts', q_rope, k_rope)
    return s.astype(jnp.float32) * MLA_SCALE


def mla_out(o_lat, w_uv):
    b, t = o_lat.shape[:2]
    return jnp.einsum('bthr,rhv->bthv', o_lat, w_uv).reshape(b, t, MLA_HEADS * MLA_V)


def mla_prompt(q_lat, q_rope, ckv, k_rope, w_uv):
    k_pos = jnp.arange(ckv.shape[1])

    def block(q_pos, qlb, qrb):
        sc = jnp.where(causal_mask(q_pos, k_pos), mla_scores(qlb, qrb, ckv, k_rope), -jnp.inf)
        p = jax.nn.softmax(sc, axis=-1).astype(ckv.dtype)
        return mla_out(jnp.einsum('bhts,bsr->bthr', p, ckv), w_uv)

    return query_blocks(block, [q_lat, q_rope])


def mla_sample(q_lat, q_rope, ckv, k_rope, ckv_past, krope_past, w_uv):
    pos = jnp.arange(q_lat.shape[1])
    s_past = mla_scores(q_lat, q_rope, ckv_past, krope_past)
    s_new = jnp.where(causal_mask(pos, pos), mla_scores(q_lat, q_rope, ckv, k_rope), -jnp.inf)
    p_past, p_new = joint_softmax(s_past, s_new)
    o_lat = (jnp.einsum('bhts,bsr->bthr', p_past.astype(ckv.dtype), ckv_past)
             + jnp.einsum('bhts,bsr->bthr', p_new.astype(ckv.dtype), ckv))
    return mla_out(o_lat, w_uv)


def _linear_combine(left, right):
    a_l, b_l = left
    a_r, b_r = right
    return a_r * a_l, a_r * b_l + b_r


def s5_mixer(u, x0, W):
    f32 = jnp.float32
    b, t = u.shape[:2]
    lam = lax.complex(W["s5_lambda_re"].astype(f32), W["s5_lambda_im"].astype(f32))
    step = jnp.exp(W["s5_log_step"].astype(f32))[:, None]
    lam_bar = jnp.exp(lam * step)
    b_bar = ((lam_bar - 1.0) / lam)[..., None] * lax.complex(W["s5_b_re"].astype(f32), W["s5_b_im"].astype(f32))
    c_mat = lax.complex(W["s5_c_re"].astype(f32), W["s5_c_im"].astype(f32))
    uf = u.astype(f32)
    ug = uf.reshape(b, t, S5_GROUPS, S5_GROUP_SIZE).astype(jnp.complex64)
    bu = jnp.einsum('gnc,btgc->btgn', b_bar, ug)
    a = jnp.broadcast_to(lam_bar, bu.shape)
    a_cum, xs = lax.associative_scan(_linear_combine, (a, bu), axis=1)
    if x0 is not None:
        xs = xs + a_cum * x0[:, None]
    y = jnp.einsum('gcn,btgn->btgc', c_mat, xs).real.reshape(b, t, S5_CH) + W["s5_d"].astype(f32) * uf
    y = jax.nn.gelu(y).astype(u.dtype)
    return y * jax.nn.sigmoid(y @ W["s5_w_glu"]), xs[:, -1]


def pre_mix(x, pos, W):
    x = x + 0.5 * swiglu_ffn(x, W["norm_ffn1"], W["w_ffn1_gate"], W["w_ffn1_up"], W["w_ffn1_down"])
    z = rmsnorm(x, W["norm_mix"]) @ W["w_in"]
    offsets = np.cumsum(SPLIT_SIZES)[:-1].tolist()
    fq, fk, fv, ff, cv, cb, cc, mq, mkv, mkr, su = jnp.split(z, offsets, axis=-1)
    b, t = x.shape[:2]
    logf = jax.nn.log_sigmoid((ff + W["fox_b_f"]).astype(jnp.float32))
    fox = (fq.reshape(b, t, FOX_HEADS, FOX_HEAD_DIM),
           fk.reshape(b, t, FOX_KV_HEADS, FOX_HEAD_DIM),
           fv.reshape(b, t, FOX_KV_HEADS, FOX_HEAD_DIM),
           logf)
    mla = mla_project(mq, mkv, mkr, pos, W["mla_norm_q"], W["mla_w_uq"], W["mla_norm_kv"], W["mla_w_uk"])
    return x, fox, (cv, cb, cc), mla, su


def post_mix(x, outs, W):
    gains = jnp.split(W["norm_group_out"], len(outs))
    y = jnp.concatenate([rmsnorm(o, g) for o, g in zip(outs, gains)], axis=-1)
    x = x + y @ W["w_out"]
    return x + 0.5 * swiglu_ffn(x, W["norm_ffn2"], W["w_ffn2_gate"], W["w_ffn2_up"], W["w_ffn2_down"])


def setup_inputs(seed: int = 0) -> dict:
    key = jax.random.key(seed)
    ks = iter(jax.random.split(key, 64))

    def nrm(shape, scale=1.0):
        return scale * jax.random.normal(next(ks), shape, jnp.float32)

    def gain(shape):
        return 1.0 + nrm(shape, 0.02)

    n_used = DEC_BATCH * N_PAGES
    n_pool = n_used + n_used // 4
    page_table = jax.random.permutation(next(ks), n_pool)[:n_used].reshape(DEC_BATCH, N_PAGES).astype(jnp.int32)
    n_idx = jnp.arange(S5_STATE, dtype=jnp.float32)
    return {
        "x_prompt": nrm((BATCH, SEQ, D_MODEL)),
        "x_sample": nrm((DEC_BATCH, DEC_SEQ, D_MODEL)),
        "cache_fox_k": nrm((DEPTH, n_pool, PAGE_SIZE, FOX_KV_HEADS, FOX_HEAD_DIM)),
        "cache_fox_v": nrm((DEPTH, n_pool, PAGE_SIZE, FOX_KV_HEADS, FOX_HEAD_DIM)),
        "cache_fox_logf": jax.nn.log_sigmoid(FOX_GATE_BIAS + nrm((DEPTH, n_pool, PAGE_SIZE, FOX_HEADS))),
        "cache_mla_ckv": nrm((DEPTH, n_pool, PAGE_SIZE, MLA_KV_RANK)),
        "cache_mla_krope": nrm((DEPTH, n_pool, PAGE_SIZE, MLA_ROPE)),
        "state_conv": nrm((DEPTH, DEC_BATCH, CONV_WIDTH - 1, CONV_CH)),
        "state_s5_re": nrm((DEPTH, DEC_BATCH, S5_GROUPS, S5_STATE), 0.5),
        "state_s5_im": nrm((DEPTH, DEC_BATCH, S5_GROUPS, S5_STATE), 0.5),
        "page_table": page_table,
        "norm_ffn1": gain((DEPTH, D_MODEL)),
        "w_ffn1_gate": nrm((DEPTH, D_MODEL, D_FF), D_MODEL ** -0.5),
        "w_ffn1_up": nrm((DEPTH, D_MODEL, D_FF), D_MODEL ** -0.5),
        "w_ffn1_down": nrm((DEPTH, D_FF, D_MODEL), D_FF ** -0.5),
        "norm_mix": gain((DEPTH, D_MODEL)),
        "w_in": nrm((DEPTH, D_MODEL, D_IN), D_MODEL ** -0.5),
        "fox_b_f": FOX_GATE_BIAS + nrm((DEPTH, FOX_HEADS), 0.1),
        "conv_w": nrm((DEPTH, CONV_WIDTH, CONV_CH), CONV_WIDTH ** -0.5),
        "mla_norm_q": gain((DEPTH, MLA_Q_RANK)),
        "mla_w_uq": nrm((DEPTH, MLA_Q_RANK, MLA_HEADS * (MLA_NOPE + MLA_ROPE)), MLA_Q_RANK ** -0.5),
        "mla_norm_kv": gain((DEPTH, MLA_KV_RANK)),
        "mla_w_uk": nrm((DEPTH, MLA_KV_RANK, MLA_HEADS, MLA_NOPE), MLA_KV_RANK ** -0.5),
        "mla_w_uv": nrm((DEPTH, MLA_KV_RANK, MLA_HEADS, MLA_V), MLA_KV_RANK ** -0.5),
        "s5_lambda_re": -0.5 + nrm((DEPTH, S5_GROUPS, S5_STATE), 0.01),
        "s5_lambda_im": math.pi * n_idx + nrm((DEPTH, S5_GROUPS, S5_STATE), 0.01),
        "s5_log_step": jax.random.uniform(next(ks), (DEPTH, S5_GROUPS), jnp.float32,
                                          math.log(S5_DT_MIN), math.log(S5_DT_MAX)),
        "s5_b_re": nrm((DEPTH, S5_GROUPS, S5_STATE, S5_GROUP_SIZE), (2 * S5_GROUP_SIZE) ** -0.5),
        "s5_b_im": nrm((DEPTH, S5_GROUPS, S5_STATE, S5_GROUP_SIZE), (2 * S5_GROUP_SIZE) ** -0.5),
        "s5_c_re": nrm((DEPTH, S5_GROUPS, S5_GROUP_SIZE, S5_STATE), (2 * S5_STATE) ** -0.5),
        "s5_c_im": nrm((DEPTH, S5_GROUPS, S5_GROUP_SIZE, S5_STATE), (2 * S5_STATE) ** -0.5),
        "s5_d": nrm((DEPTH, S5_CH)),
        "s5_w_glu": nrm((DEPTH, S5_CH, S5_CH), S5_CH ** -0.5),
        "norm_group_out": gain((DEPTH, D_MIX)),
        "w_out": nrm((DEPTH, D_MIX, D_MODEL), D_MIX ** -0.5),
        "norm_ffn2": gain((DEPTH, D_MODEL)),
        "w_ffn2_gate": nrm((DEPTH, D_MODEL, D_FF), D_MODEL ** -0.5),
        "w_ffn2_up": nrm((DEPTH, D_MODEL, D_FF), D_MODEL ** -0.5),
        "w_ffn2_down": nrm((DEPTH, D_FF, D_MODEL), D_FF ** -0.5),
        "norm_final": gain((D_MODEL,)),
    }


def reference(x_prompt, x_sample, cache_fox_k, cache_fox_v, cache_fox_logf, cache_mla_ckv, cache_mla_krope,
              state_conv, state_s5_re, state_s5_im, page_table,
              norm_ffn1, w_ffn1_gate, w_ffn1_up, w_ffn1_down, norm_mix, w_in, fox_b_f, conv_w,
              mla_norm_q, mla_w_uq, mla_norm_kv, mla_w_uk, mla_w_uv,
              s5_lambda_re, s5_lambda_im, s5_log_step, s5_b_re, s5_b_im, s5_c_re, s5_c_im, s5_d, s5_w_glu,
              norm_group_out, w_out, norm_ffn2, w_ffn2_gate, w_ffn2_up, w_ffn2_down, norm_final):
    f32 = jnp.float32
    bp, sp = x_prompt.shape[:2]
    sd = x_sample.shape[1]
    n_past = page_table.shape[1] * cache_fox_k.shape[2]
    pos_p = jnp.arange(sp)
    pos_d = n_past + jnp.arange(sd)
    xp, xd = x_prompt, x_sample
    acc = {name: [] for name in STATE_KEYS}
    for l in range(DEPTH):
        W = {
            "norm_ffn1": norm_ffn1[l], "w_ffn1_gate": w_ffn1_gate[l], "w_ffn1_up": w_ffn1_up[l],
            "w_ffn1_down": w_ffn1_down[l], "norm_mix": norm_mix[l], "w_in": w_in[l], "fox_b_f": fox_b_f[l],
            "mla_norm_q": mla_norm_q[l], "mla_w_uq": mla_w_uq[l], "mla_norm_kv": mla_norm_kv[l],
            "mla_w_uk": mla_w_uk[l],
            "s5_lambda_re": s5_lambda_re[l], "s5_lambda_im": s5_lambda_im[l], "s5_log_step": s5_log_step[l],
            "s5_b_re": s5_b_re[l], "s5_b_im": s5_b_im[l], "s5_c_re": s5_c_re[l], "s5_c_im": s5_c_im[l],
            "s5_d": s5_d[l], "s5_w_glu": s5_w_glu[l],
            "norm_group_out": norm_group_out[l], "w_out": w_out[l], "norm_ffn2": norm_ffn2[l],
            "w_ffn2_gate": w_ffn2_gate[l], "w_ffn2_up": w_ffn2_up[l], "w_ffn2_down": w_ffn2_down[l],
        }
        xp, fox_p, conv_in_p, mla_p, su_p = pre_mix(xp, pos_p, W)
        a_p = fox_prompt(*fox_p)
        zero_prefix = jnp.zeros((bp, CONV_WIDTH - 1, CONV_CH), xp.dtype)
        b_p, conv_state_p = short_conv_mixer(*conv_in_p, zero_prefix, conv_w[l])
        c_p = mla_prompt(*mla_p, mla_w_uv[l])
        d_p, s5_state_p = s5_mixer(su_p, None, W)
        xp = post_mix(xp, [a_p, b_p, c_p, d_p], W)
        xd, fox_d, conv_in_d, mla_d, su_d = pre_mix(xd, pos_d, W)
        a_d = fox_sample(*fox_d, paged_rows(cache_fox_k, l, page_table), paged_rows(cache_fox_v, l, page_table),
                         paged_rows(cache_fox_logf, l, page_table))
        b_d, conv_state_d = short_conv_mixer(*conv_in_d, state_conv[l], conv_w[l])
        c_d = mla_sample(*mla_d, paged_rows(cache_mla_ckv, l, page_table),
                         paged_rows(cache_mla_krope, l, page_table), mla_w_uv[l])
        x0 = lax.complex(state_s5_re[l].astype(f32), state_s5_im[l].astype(f32))
        d_d, s5_state_d = s5_mixer(su_d, x0, W)
        xd = post_mix(xd, [a_d, b_d, c_d, d_d], W)
        acc["fox_k_p"].append(fox_p[1])
        acc["fox_v_p"].append(fox_p[2])
        acc["fox_logf_p"].append(fox_p[3].astype(cache_fox_logf.dtype))
        acc["mla_ckv_p"].append(mla_p[2])
        acc["mla_krope_p"].append(mla_p[3])
        acc["conv_p"].append(conv_state_p)
        acc["s5_re_p"].append(jnp.real(s5_state_p).astype(state_s5_re.dtype))
        acc["s5_im_p"].append(jnp.imag(s5_state_p).astype(state_s5_im.dtype))
        acc["fox_k_s"].append(fox_d[1])
        acc["fox_v_s"].append(fox_d[2])
        acc["fox_logf_s"].append(fox_d[3].astype(cache_fox_logf.dtype))
        acc["mla_ckv_s"].append(mla_d[2])
        acc["mla_krope_s"].append(mla_d[3])
        acc["conv_s"].append(conv_state_d)
        acc["s5_re_s"].append(jnp.real(s5_state_d).astype(state_s5_re.dtype))
        acc["s5_im_s"].append(jnp.imag(s5_state_d).astype(state_s5_im.dtype))
    y_prompt = rmsnorm(xp, norm_final)
    y_sample = rmsnorm(xd, norm_final)
    st = {name: jnp.stack(vals) for name, vals in acc.items()}
    return (y_prompt, y_sample,
            st["fox_k_p"], st["fox_v_p"], st["fox_logf_p"], st["mla_ckv_p"], st["mla_krope_p"],
            st["conv_p"], st["s5_re_p"], st["s5_im_p"],
            st["fox_k_s"], st["fox_v_s"], st["fox_logf_s"], st["mla_ckv_s"], st["mla_krope_s"],
            st["conv_s"], st["s5_re_s"], st["s5_im_s"])
```

```python
import functools
import math

import jax
import jax.numpy as jnp
import numpy as np
from jax import lax
from jax.experimental import pallas as pl
from jax.experimental.pallas import tpu as pltpu

F32 = jnp.float32
BF16 = jnp.bfloat16

EPS = 1e-6
FOX_HEADS = 4
FOX_HEAD_DIM = 128
FOX_SCALE = FOX_HEAD_DIM ** -0.5
GROUP_WIDTH = 512
MLA_HEADS = 4
MLA_NOPE = 128
MLA_ROPE = 64
MLA_V = 128
MLA_Q_RANK = 512
MLA_KV_RANK = 256
MLA_SCALE = (MLA_NOPE + MLA_ROPE) ** -0.5
ROPE_THETA = 10000.0
S5_GROUPS = 32
S5_GROUP_SIZE = 16
S5_STATE = 64
S5_LANES = S5_GROUPS * S5_STATE
CONV_WIDTH = 3
LANE = 128
SUBLANE = 8
HEAD_ROWS = 8
VMEM_LIMIT = 56 * 1024 * 1024

_SEG = dict(fq=(0, 512), fkv=(512, 256), cv=(768, 512), cb=(1280, 512), cc=(1792, 512),
            mq=(2304, 512), mkv=(2816, 256), su=(3072, 512), kr=(3584, 128), ff=(3712, 128))
D_IN_PAD = 3840


def _cparams(sem, vmem=VMEM_LIMIT):
    return pltpu.CompilerParams(dimension_semantics=sem, vmem_limit_bytes=vmem)


def _rms(x, g):
    ms = jnp.mean(x * x, axis=-1, keepdims=True)
    return x * lax.rsqrt(ms + EPS) * g


def _dot(a, b):
    return jnp.dot(a, b, preferred_element_type=F32)


def _dot_nt(a, b):
    return lax.dot_general(a, b, (((1,), (1,)), ((), ())), preferred_element_type=F32)


def _const_spec(shape):
    nd = len(shape)
    return pl.BlockSpec(shape, lambda *_: (0,) * nd, pipeline_mode=pl.Buffered(1))


def _ffn_body(x_ref, g_ref, wg_ref, wu_ref, wd_ref, o_ref, h_ref):
    f = pl.program_id(1)

    @pl.when(f == 0)
    def _():
        x = x_ref[...]
        h_ref[...] = _rms(x, g_ref[...]).astype(BF16)
        o_ref[...] = x

    h = h_ref[...]
    a = _dot(h, wg_ref[...])
    b = _dot(h, wu_ref[...])
    act = (a * jax.nn.sigmoid(a) * b).astype(BF16)
    o_ref[...] += 0.5 * _dot(act, wd_ref[...])


def _ffn(x, g, wg, wu, wd, *, tm, tf):
    m, d = x.shape
    fpad = wg.shape[1]
    return pl.pallas_call(
        _ffn_body,
        out_shape=jax.ShapeDtypeStruct((m, d), F32),
        grid=(m // tm, fpad // tf),
        in_specs=[
            pl.BlockSpec((tm, d), lambda i, f: (i, 0)),
            pl.BlockSpec((1, d), lambda i, f: (0, 0)),
            pl.BlockSpec((d, tf), lambda i, f: (0, f)),
            pl.BlockSpec((d, tf), lambda i, f: (0, f)),
            pl.BlockSpec((tf, d), lambda i, f: (f, 0)),
        ],
        out_specs=pl.BlockSpec((tm, d), lambda i, f: (i, 0)),
        scratch_shapes=[pltpu.VMEM((tm, d), BF16)],
        compiler_params=_cparams(("parallel", "arbitrary")),
        name="ffn",
    )(x, g, wg, wu, wd)


def _inproj_body(x_ref, g_ref, w_ref, bias_ref, fq_ref, fk_ref, fv_ref, cv_ref, cb_ref, cc_ref,
                 mq_ref, mkv_ref, su_ref, kr_ref, lf_ref):
    h = _rms(x_ref[...], g_ref[...]).astype(BF16)

    def seg(name):
        a, n = _SEG[name]
        return _dot(h, w_ref[:, a:a + n])

    fq_ref[...] = seg("fq")
    fkv = seg("fkv")
    fk_ref[...] = fkv[:, :FOX_HEAD_DIM]
    fv_ref[...] = fkv[:, FOX_HEAD_DIM:]
    cv_ref[...] = seg("cv")
    cb_ref[...] = seg("cb")
    cc_ref[...] = seg("cc")
    mq_ref[...] = seg("mq")
    mkv_ref[...] = seg("mkv")
    su_ref[...] = seg("su")
    kr_ref[...] = seg("kr")
    zf = seg("ff") + bias_ref[...]
    lf_ref[...] = -(jnp.maximum(-zf, 0.0) + jnp.log1p(jnp.exp(-jnp.abs(zf))))


def _inproj(x, g, w, bias, *, tm):
    m, d = x.shape
    widths = [512, 128, 128, 512, 512, 512, 512, 256, 512, 128, 128]
    return pl.pallas_call(
        _inproj_body,
        out_shape=[jax.ShapeDtypeStruct((m, n), F32) for n in widths],
        grid=(m // tm,),
        in_specs=[
            pl.BlockSpec((tm, d), lambda i: (i, 0)),
            pl.BlockSpec((1, d), lambda i: (0, 0)),
            _const_spec(w.shape),
            pl.BlockSpec((1, LANE), lambda i: (0, 0)),
        ],
        out_specs=[pl.BlockSpec((tm, n), lambda i: (i, 0)) for n in widths],
        compiler_params=_cparams(("parallel",)),
        name="inproj",
    )(x, g, w, bias)


def _cumsum_body(x_ref, o_ref):
    x = x_ref[...]
    n = x.shape[1]
    lane = lax.broadcasted_iota(jnp.int32, x.shape, 1)
    d = 1
    while d < n:
        x = x + jnp.where(lane >= d, pltpu.roll(x, d, axis=1), 0.0)
        d *= 2
    o_ref[...] = x


def _cumsum_lanes(x):
    return pl.pallas_call(
        _cumsum_body, out_shape=jax.ShapeDtypeStruct(x.shape, F32), name="logf_cumsum")(x)


def _fox_prompt_body(q_ref, k_ref, v_ref, cq_ref, ck_ref, o_ref, m_ref, l_ref, acc_ref, *, tq, tk):
    i = pl.program_id(1)
    j = pl.program_id(2)

    @pl.when(j == 0)
    def _():
        m_ref[...] = jnp.full(m_ref.shape, -jnp.inf, F32)
        l_ref[...] = jnp.zeros(l_ref.shape, F32)
        acc_ref[...] = jnp.zeros(acc_ref.shape, F32)

    @pl.when(j <= i)
    def _():
        kb = k_ref[...].astype(BF16)
        vb = v_ref[...].astype(BF16)
        row = i * tq + lax.broadcasted_iota(jnp.int32, (tq, tk), 0)
        col = j * tk + lax.broadcasted_iota(jnp.int32, (tq, tk), 1)
        mask = col <= row
        for h in range(FOX_HEADS):
            hs = slice(h * FOX_HEAD_DIM, (h + 1) * FOX_HEAD_DIM)
            s = _dot_nt(q_ref[:, hs].astype(BF16), kb) * FOX_SCALE
            s = s + cq_ref[:, h:h + 1] - ck_ref[0, h:h + 1, :]
            s = jnp.where(mask, s, -jnp.inf)
            m_prev = m_ref[h]
            m_new = jnp.maximum(m_prev, jnp.max(s, axis=-1, keepdims=True))
            alpha = jnp.exp(m_prev - m_new)
            p = jnp.exp(s - m_new)
            l_ref[h] = alpha * l_ref[h] + jnp.sum(p, axis=-1, keepdims=True)
            acc_ref[:, hs] = alpha * acc_ref[:, hs] + _dot(p.astype(BF16), vb)
            m_ref[h] = m_new

    @pl.when(j == pl.num_programs(2) - 1)
    def _():
        for h in range(FOX_HEADS):
            hs = slice(h * FOX_HEAD_DIM, (h + 1) * FOX_HEAD_DIM)
            o_ref[:, hs] = acc_ref[:, hs] / l_ref[h]


def _fox_prompt(q, k, v, cq, ck, *, nb, seq, tq, tk):
    nq, nk = seq // tq, seq // tk
    kv_idx = lambda b, i, j: (b * nk + jnp.minimum(j, i), 0)
    return pl.pallas_call(
        functools.partial(_fox_prompt_body, tq=tq, tk=tk),
        out_shape=jax.ShapeDtypeStruct(q.shape, F32),
        grid=(nb, nq, nk),
        in_specs=[
            pl.BlockSpec((tq, q.shape[1]), lambda b, i, j: (b * nq + i, 0)),
            pl.BlockSpec((tk, FOX_HEAD_DIM), kv_idx),
            pl.BlockSpec((tk, FOX_HEAD_DIM), kv_idx),
            pl.BlockSpec((tq, FOX_HEADS), lambda b, i, j: (b * nq + i, 0)),
            pl.BlockSpec((1, FOX_HEADS, tk), lambda b, i, j: (b, 0, jnp.minimum(j, i))),
        ],
        out_specs=pl.BlockSpec((tq, q.shape[1]), lambda b, i, j: (b * nq + i, 0)),
        scratch_shapes=[
            pltpu.VMEM((FOX_HEADS, tq, 1), F32),
            pltpu.VMEM((FOX_HEADS, tq, 1), F32),
            pltpu.VMEM((tq, q.shape[1]), F32),
        ],
        compiler_params=_cparams(("parallel", "parallel", "arbitrary")),
        name="fox_prompt",
    )(q, k, v, cq, ck)


def _mla_proj_body(mq_ref, mkv_ref, kr_ref, cos_ref, sin_ref, gq_ref, gkv_ref, wn_ref, wr_ref,
                   wrr_ref, wuk_ref, ql_ref, qr_ref, ckv_ref, ckvb_ref, krope_ref, kropeb_ref):
    cq = _rms(mq_ref[...], gq_ref[...]).astype(BF16)
    cos = cos_ref[...]
    sin = sin_ref[...]
    qr_ref[...] = (_dot(cq, wr_ref[...]) * cos + _dot(cq, wrr_ref[...]) * sin).astype(BF16)
    qn = _dot(cq, wn_ref[...])
    for h in range(MLA_HEADS):
        qh = qn[:, h * MLA_NOPE:(h + 1) * MLA_NOPE].astype(BF16)
        ql_ref[:, h * MLA_KV_RANK:(h + 1) * MLA_KV_RANK] = _dot(qh, wuk_ref[h]).astype(BF16)
    ckv = _rms(mkv_ref[...], gkv_ref[...])
    ckv_ref[...] = ckv
    ckvb_ref[...] = ckv.astype(BF16)
    kr = kr_ref[...]
    krope = kr[:, :MLA_ROPE] * cos[:, :MLA_ROPE] + kr[:, MLA_ROPE:] * sin[:, :MLA_ROPE]
    krope_ref[...] = krope
    kropeb_ref[...] = krope.astype(BF16)


def _mla_proj(mq, mkv, kr, cos, sin, gq, gkv, wn, wr, wrr, wuk, *, tm):
    m = mq.shape[0]
    ntab = cos.shape[0] // tm
    row = lambda n: pl.BlockSpec((tm, n), lambda i: (i, 0))
    tab = pl.BlockSpec((tm, cos.shape[1]), lambda i: (i % ntab, 0))
    return pl.pallas_call(
        _mla_proj_body,
        out_shape=[
            jax.ShapeDtypeStruct((m, MLA_HEADS * MLA_KV_RANK), BF16),
            jax.ShapeDtypeStruct((m, MLA_HEADS * MLA_ROPE), BF16),
            jax.ShapeDtypeStruct((m, MLA_KV_RANK), F32),
            jax.ShapeDtypeStruct((m, MLA_KV_RANK), BF16),
            jax.ShapeDtypeStruct((m, MLA_ROPE), F32),
            jax.ShapeDtypeStruct((m, MLA_ROPE), BF16),
        ],
        grid=(m // tm,),
        in_specs=[row(MLA_Q_RANK), row(MLA_KV_RANK), row(LANE), tab, tab,
                  pl.BlockSpec((1, MLA_Q_RANK), lambda i: (0, 0)),
                  pl.BlockSpec((1, MLA_KV_RANK), lambda i: (0, 0)),
                  _const_spec(wn.shape), _const_spec(wr.shape), _const_spec(wrr.shape),
                  _const_spec(wuk.shape)],
        out_specs=[row(MLA_HEADS * MLA_KV_RANK), row(MLA_HEADS * MLA_ROPE), row(MLA_KV_RANK),
                   row(MLA_KV_RANK), row(MLA_ROPE), row(MLA_ROPE)],
        compiler_params=_cparams(("parallel",)),
        name="mla_proj",
    )(mq, mkv, kr, cos, sin, gq, gkv, wn, wr, wrr, wuk)


def _mla_prompt_body(ql_ref, qr_ref, ckv_ref, kr_ref, wuv_ref, o_ref, m_ref, l_ref, acc_ref,
                     *, tq, tk):
    i = pl.program_id(1)
    j = pl.program_id(2)

    @pl.when(j == 0)
    def _():
        m_ref[...] = jnp.full(m_ref.shape, -jnp.inf, F32)
        l_ref[...] = jnp.zeros(l_ref.shape, F32)
        acc_ref[...] = jnp.zeros(acc_ref.shape, F32)

    @pl.when(j <= i)
    def _():
        cb = ckv_ref[...]
        kb = kr_ref[...]
        row = i * tq + lax.broadcasted_iota(jnp.int32, (tq, tk), 0)
        col = j * tk + lax.broadcasted_iota(jnp.int32, (tq, tk), 1)
        mask = col <= row
        for h in range(MLA_HEADS):
            ls = slice(h * MLA_KV_RANK, (h + 1) * MLA_KV_RANK)
            rs = slice(h * MLA_ROPE, (h + 1) * MLA_ROPE)
            s = (_dot_nt(ql_ref[:, ls], cb) + _dot_nt(qr_ref[:, rs], kb)) * MLA_SCALE
            s = jnp.where(mask, s, -jnp.inf)
            m_prev = m_ref[h]
            m_new = jnp.maximum(m_prev, jnp.max(s, axis=-1, keepdims=True))
            alpha = jnp.exp(m_prev - m_new)
            p = jnp.exp(s - m_new)
            l_ref[h] = alpha * l_ref[h] + jnp.sum(p, axis=-1, keepdims=True)
            acc_ref[:, ls] = alpha * acc_ref[:, ls] + _dot(p.astype(BF16), cb)
            m_ref[h] = m_new

    @pl.when(j == pl.num_programs(2) - 1)
    def _():
        for h in range(MLA_HEADS):
            ls = slice(h * MLA_KV_RANK, (h + 1) * MLA_KV_RANK)
            o_lat = (acc_ref[:, ls] / l_ref[h]).astype(BF16)
            o_ref[:, h * MLA_V:(h + 1) * MLA_V] = _dot(o_lat, wuv_ref[h])


def _mla_prompt(ql, qr, ckvb, krb, wuv, *, nb, seq, tq, tk):
    nq, nk = seq // tq, seq // tk
    kv_idx = lambda b, i, j: (b * nk + jnp.minimum(j, i), 0)
    m = ql.shape[0]
    return pl.pallas_call(
        functools.partial(_mla_prompt_body, tq=tq, tk=tk),
        out_shape=jax.ShapeDtypeStruct((m, MLA_HEADS * MLA_V), F32),
        grid=(nb, nq, nk),
        in_specs=[
            pl.BlockSpec((tq, ql.shape[1]), lambda b, i, j: (b * nq + i, 0)),
            pl.BlockSpec((tq, qr.shape[1]), lambda b, i, j: (b * nq + i, 0)),
            pl.BlockSpec((tk, MLA_KV_RANK), kv_idx),
            pl.BlockSpec((tk, MLA_ROPE), kv_idx),
            _const_spec(wuv.shape),
        ],
        out_specs=pl.BlockSpec((tq, MLA_HEADS * MLA_V), lambda b, i, j: (b * nq + i, 0)),
        scratch_shapes=[
            pltpu.VMEM((MLA_HEADS, tq, 1), F32),
            pltpu.VMEM((MLA_HEADS, tq, 1), F32),
            pltpu.VMEM((tq, MLA_HEADS * MLA_KV_RANK), F32),
        ],
        compiler_params=_cparams(("parallel", "parallel", "arbitrary")),
        name="mla_prompt",
    )(ql, qr, ckvb, krb, wuv)


def _gelu_tanh(y):
    return 0.5 * y * (1.0 + jnp.tanh(math.sqrt(2.0 / math.pi) * (y + 0.044715 * (y * y * y))))


def _s5_prompt_body(u_ref, bre_ref, bim_ref, cre_ref, cim_ref, pre_ref, pim_ref, d_ref, wg_ref,
                    o_ref, sre_ref, sim_ref, xre_ref, xim_ref, car_re, car_im, *, lt):
    t = pl.program_id(1)
    u = u_ref[...]
    ub = u.astype(BF16)
    xre_ref[...] = _dot(ub, bre_ref[...])
    xim_ref[...] = _dot(ub, bim_ref[...])

    @pl.when(t == 0)
    def _():
        car_re[...] = jnp.zeros(car_re.shape, F32)
        car_im[...] = jnp.zeros(car_im.shape, F32)

    first = lax.broadcasted_iota(jnp.int32, (SUBLANE, S5_LANES), 0) == 0
    lre = pre_ref[0:1, :]
    lim = pim_ref[0:1, :]
    cre = car_re[...]
    cim = car_im[...]
    xre_ref[0:SUBLANE, :] += jnp.where(first, lre * cre - lim * cim, 0.0)
    xim_ref[0:SUBLANE, :] += jnp.where(first, lre * cim + lim * cre, 0.0)

    nsteps = int(math.log2(lt))
    rowid = lax.broadcasted_iota(jnp.int32, (lt, LANE), 0)

    def lane_block(c, carry):
        off = pl.multiple_of(c * LANE, LANE)
        xr = xre_ref[:, pl.ds(off, LANE)]
        xi = xim_ref[:, pl.ds(off, LANE)]
        for k in range(nsteps):
            d = 1 << k
            ar = pre_ref[pl.ds(k, 1), pl.ds(off, LANE)]
            ai = pim_ref[pl.ds(k, 1), pl.ds(off, LANE)]
            keep = rowid >= d
            sr = jnp.where(keep, pltpu.roll(xr, d, axis=0), 0.0)
            si = jnp.where(keep, pltpu.roll(xi, d, axis=0), 0.0)
            xr, xi = xr + (ar * sr - ai * si), xi + (ar * si + ai * sr)
        xre_ref[:, pl.ds(off, LANE)] = xr
        xim_ref[:, pl.ds(off, LANE)] = xi
        return carry

    lax.fori_loop(0, S5_LANES // LANE, lane_block, 0)

    last_re = xre_ref[lt - SUBLANE:lt, :]
    last_im = xim_ref[lt - SUBLANE:lt, :]
    car_re[...] = jnp.broadcast_to(last_re[SUBLANE - 1:SUBLANE, :], car_re.shape)
    car_im[...] = jnp.broadcast_to(last_im[SUBLANE - 1:SUBLANE, :], car_im.shape)
    sre_ref[0] = last_re
    sim_ref[0] = last_im

    y = (_dot(xre_ref[...].astype(BF16), cre_ref[...]) + _dot(xim_ref[...].astype(BF16), cim_ref[...])
         + d_ref[...] * u)
    y = _gelu_tanh(y)
    o_ref[...] = y * jax.nn.sigmoid(_dot(y.astype(BF16), wg_ref[...]))


def _s5_prompt(u, bre, bim, cre, cim, pre, pim, dvec, wglu, *, nb, seq, lt):
    m = u.shape[0]
    nt = seq // lt
    return pl.pallas_call(
        functools.partial(_s5_prompt_body, lt=lt),
        out_shape=[
            jax.ShapeDtypeStruct((m, GROUP_WIDTH), F32),
            jax.ShapeDtypeStruct((nb, SUBLANE, S5_LANES), F32),
            jax.ShapeDtypeStruct((nb, SUBLANE, S5_LANES), F32),
        ],
        grid=(nb, nt),
        in_specs=[
            pl.BlockSpec((lt, GROUP_WIDTH), lambda b, t: (b * nt + t, 0)),
            _const_spec(bre.shape), _const_spec(bim.shape),
            _const_spec(cre.shape), _const_spec(cim.shape),
            _const_spec(pre.shape), _const_spec(pim.shape),
            pl.BlockSpec((1, GROUP_WIDTH), lambda b, t: (0, 0)),
            _const_spec(wglu.shape),
        ],
        out_specs=[
            pl.BlockSpec((lt, GROUP_WIDTH), lambda b, t: (b * nt + t, 0)),
            pl.BlockSpec((1, SUBLANE, S5_LANES), lambda b, t: (b, 0, 0)),
            pl.BlockSpec((1, SUBLANE, S5_LANES), lambda b, t: (b, 0, 0)),
        ],
        scratch_shapes=[
            pltpu.VMEM((lt, S5_LANES), F32), pltpu.VMEM((lt, S5_LANES), F32),
            pltpu.VMEM((SUBLANE, S5_LANES), F32), pltpu.VMEM((SUBLANE, S5_LANES), F32),
        ],
        compiler_params=_cparams(("parallel", "arbitrary")),
        name="s5_prompt",
    )(u, bre, bim, cre, cim, pre, pim, dvec, wglu)


def _group_out(x, outs, gains_ref, wout_ref):
    y = x
    for g, o in enumerate(outs):
        n = _rms(o, gains_ref[g:g + 1, :]).astype(BF16)
        y = y + _dot(n, wout_ref[g * GROUP_WIDTH:(g + 1) * GROUP_WIDTH, :])
    return y


def _postmix_prompt_body(x_ref, a_ref, cv_ref, cb_ref, cc_ref, hv_ref, hc_ref, c_ref, d_ref,
                         gains_ref, cw_ref, wout_ref, o_ref, cs_ref, *, tm, tiles_per_seq):
    i = pl.program_id(0)
    xc = cc_ref[...] * cv_ref[...]
    halo = hc_ref[...] * hv_ref[...]
    halo = jnp.where(i % tiles_per_seq == 0, 0.0, halo)
    rowid = lax.broadcasted_iota(jnp.int32, xc.shape, 0)
    h7 = halo[SUBLANE - 1:SUBLANE, :]
    h6 = halo[SUBLANE - 2:SUBLANE - 1, :]
    prev1 = jnp.where(rowid == 0, h7, pltpu.roll(xc, 1, axis=0))
    prev2 = jnp.where(rowid == 0, h6, jnp.where(rowid == 1, h7, pltpu.roll(xc, 2, axis=0)))
    conv = cw_ref[0:1, :] * prev2 + cw_ref[1:2, :] * prev1 + cw_ref[2:3, :] * xc
    b = cb_ref[...] * conv
    cs_ref[0] = xc[tm - SUBLANE:tm, :]
    o_ref[...] = _group_out(x_ref[...], [a_ref[...], b, c_ref[...], d_ref[...]], gains_ref, wout_ref)


def _postmix_prompt(x, a, cv, cb, cc, c, d, gains, cw, wout, *, seq, tm):
    m, dm = x.shape
    tiles_per_seq = seq // tm
    nb = m // seq
    row = lambda n: pl.BlockSpec((tm, n), lambda i: (i, 0))
    halo = pl.BlockSpec((SUBLANE, GROUP_WIDTH),
                        lambda i: (jnp.maximum(i * (tm // SUBLANE) - 1, 0), 0))
    return pl.pallas_call(
        functools.partial(_postmix_prompt_body, tm=tm, tiles_per_seq=tiles_per_seq),
        out_shape=[jax.ShapeDtypeStruct((m, dm), F32),
                   jax.ShapeDtypeStruct((nb, SUBLANE, GROUP_WIDTH), F32)],
        grid=(m // tm,),
        in_specs=[row(dm), row(GROUP_WIDTH), row(GROUP_WIDTH), row(GROUP_WIDTH), row(GROUP_WIDTH),
                  halo, halo, row(GROUP_WIDTH), row(GROUP_WIDTH),
                  pl.BlockSpec(gains.shape, lambda i: (0, 0)),
                  pl.BlockSpec(cw.shape, lambda i: (0, 0)),
                  _const_spec(wout.shape)],
        out_specs=[row(dm),
                   pl.BlockSpec((1, SUBLANE, GROUP_WIDTH), lambda i: (i // tiles_per_seq, 0, 0))],
        compiler_params=_cparams(("arbitrary",)),
        name="postmix_prompt",
    )(x, a, cv, cb, cc, cv, cc, c, d, gains, cw, wout)


def _split3(x):
    hi = x.astype(BF16)
    r1 = x - hi.astype(F32)
    mid = r1.astype(BF16)
    lo = (r1 - mid.astype(F32)).astype(BF16)
    return hi, mid, lo


def _postmix_sample_body(x_ref, a_ref, cv_ref, cb_ref, cc_ref, s0_ref, s1_ref, olat_ref, u_ref,
                         x0re_ref, x0im_ref, gains_ref, cw_ref, wout_ref, wuv_ref,
                         bre_hi_ref, bre_lo_ref, bim_hi_ref, bim_lo_ref, cre_ref, cim_ref,
                         lre_ref, lim_ref, d_ref, wg_ref,
                         o_ref, xc_ref, sre_ref, sim_ref):
    xc = cc_ref[...] * cv_ref[...]
    conv = cw_ref[0:1, :] * s0_ref[...] + cw_ref[1:2, :] * s1_ref[...] + cw_ref[2:3, :] * xc
    b = cb_ref[...] * conv
    xc_ref[...] = xc
    cs = []
    for h in range(MLA_HEADS):
        ol = olat_ref[:, h * MLA_KV_RANK:(h + 1) * MLA_KV_RANK].astype(BF16)
        cs.append(_dot(ol, wuv_ref[h]))
    c = jnp.concatenate(cs, axis=1)
    u = u_ref[...]
    u_hi, u_mid, _ = _split3(u)
    bu_re = (_dot(u_hi, bre_hi_ref[...]) + _dot(u_mid, bre_hi_ref[...]) + _dot(u_hi, bre_lo_ref[...]))
    bu_im = (_dot(u_hi, bim_hi_ref[...]) + _dot(u_mid, bim_hi_ref[...]) + _dot(u_hi, bim_lo_ref[...]))
    lre = lre_ref[...]
    lim = lim_ref[...]
    x0re = x0re_ref[...]
    x0im = x0im_ref[...]
    nre = lre * x0re - lim * x0im + bu_re
    nim = lre * x0im + lim * x0re + bu_im
    sre_ref[...] = nre
    sim_ref[...] = nim
    y = _dot(nre.astype(BF16), cre_ref[...]) + _dot(nim.astype(BF16), cim_ref[...]) + d_ref[...] * u
    y = _gelu_tanh(y)
    d = y * jax.nn.sigmoid(_dot(y.astype(BF16), wg_ref[...]))
    o_ref[...] = _group_out(x_ref[...], [a_ref[...], b, c, d], gains_ref, wout_ref)


def _postmix_sample(x, a, cv, cb, cc, s0, s1, olat, u, x0re, x0im, gains, cw, wout, wuv,
                    bre_hi, bre_lo, bim_hi, bim_lo, cre, cim, lre, lim, dvec, wglu):
    m, dm = x.shape
    return pl.pallas_call(
        _postmix_sample_body,
        out_shape=[jax.ShapeDtypeStruct((m, dm), F32),
                   jax.ShapeDtypeStruct((m, GROUP_WIDTH), F32),
                   jax.ShapeDtypeStruct((m, S5_LANES), F32),
                   jax.ShapeDtypeStruct((m, S5_LANES), F32)],
        compiler_params=pltpu.CompilerParams(vmem_limit_bytes=VMEM_LIMIT),
        name="postmix_sample",
    )(x, a, cv, cb, cc, s0, s1, olat, u, x0re, x0im, gains, cw, wout, wuv,
      bre_hi, bre_lo, bim_hi, bim_lo, cre, cim, lre, lim, dvec, wglu)


def _fox_decode_body(pt_ref, q_ref, kn_ref, vn_ref, lfn_ref, gpre_ref, tpre_ref,
                     k_hbm, v_hbm, lf_hbm, o_ref, kbuf, vbuf, lfbuf, sems, *, npages, page):
    b = pl.program_id(0)
    nb = pl.num_programs(0)
    slot = b % 2

    def copies(seq, sl, fn):
        def body(pg, carry):
            pid = pt_ref[seq, pg]
            row = pl.multiple_of(pg * page, page)
            hrow = pl.multiple_of(pg * HEAD_ROWS, HEAD_ROWS)
            fn(pltpu.make_async_copy(k_hbm.at[pid], kbuf.at[sl, pl.ds(row, page), :], sems.at[sl, 0]))
            fn(pltpu.make_async_copy(v_hbm.at[pid], vbuf.at[sl, pl.ds(row, page), :], sems.at[sl, 1]))
            fn(pltpu.make_async_copy(lf_hbm.at[pid], lfbuf.at[sl, pl.ds(hrow, FOX_HEADS), :],
                                     sems.at[sl, 2]))
            return carry
        lax.fori_loop(0, npages, body, 0)

    @pl.when(b == 0)
    def _():
        lfbuf[...] = jnp.zeros(lfbuf.shape, F32)
        copies(0, 0, lambda c: c.start())

    @pl.when(b + 1 < nb)
    def _():
        copies(b + 1, 1 - slot, lambda c: c.start())

    copies(b, slot, lambda c: c.wait())

    rows = npages * HEAD_ROWS
    gpre = gpre_ref[...]
    wp = sum(_dot(part, gpre) for part in _split3(lfbuf[slot]))
    tot = jnp.broadcast_to(wp[:, page - 1:page], (rows, page))
    tpre = tpre_ref[...]
    off = sum(_dot(tpre, part) for part in _split3(tot))
    cum = wp + off
    total = cum[rows - HEAD_ROWS:rows, page - 1:page]
    bias0 = lfn_ref[0] + total

    q8 = q_ref[0]
    qb = q8.astype(BF16)
    kb = kbuf[slot].astype(BF16)
    s_all = _dot_nt(qb, kb) * FOX_SCALE
    s_new = jnp.sum(q8 * kn_ref[0], axis=-1, keepdims=True) * FOX_SCALE
    zs = []
    mx = jnp.full((HEAD_ROWS, page), -jnp.inf, F32)
    for pg in range(npages):
        z = s_all[:, pg * page:(pg + 1) * page] + (bias0 - cum[pg * HEAD_ROWS:(pg + 1) * HEAD_ROWS, :])
        zs.append(z)
        mx = jnp.maximum(mx, z)
    m = jnp.maximum(jnp.max(mx, axis=-1, keepdims=True), s_new)
    ps = [jnp.exp(z - m) for z in zs]
    lsum = ps[0]
    for p in ps[1:]:
        lsum = lsum + p
    p_new = jnp.exp(s_new - m)
    l = jnp.sum(lsum, axis=-1, keepdims=True) + p_new
    pcat = jnp.concatenate([p.astype(BF16) for p in ps], axis=1)
    o = _dot(pcat, vbuf[slot].astype(BF16)) + p_new * vn_ref[0]
    o_ref[0] = o / l


def _fox_decode(pt, q8, kn, vn, lfn, gpre, tpre, kc, vc, lfc, *, npages, page):
    nseq = q8.shape[0]
    hd = FOX_HEAD_DIM
    blk = lambda r, n: pl.BlockSpec((1, r, n), lambda b, pt_ref: (b, 0, 0))
    const = lambda a: pl.BlockSpec(a.shape, lambda b, pt_ref: (0,) * a.ndim)
    grid_spec = pltpu.PrefetchScalarGridSpec(
        num_scalar_prefetch=1,
        grid=(nseq,),
        in_specs=[blk(HEAD_ROWS, hd), blk(1, hd), blk(1, hd), blk(HEAD_ROWS, page),
                  const(gpre), const(tpre),
                  pl.BlockSpec(memory_space=pl.ANY), pl.BlockSpec(memory_space=pl.ANY),
                  pl.BlockSpec(memory_space=pl.ANY)],
        out_specs=blk(HEAD_ROWS, hd),
        scratch_shapes=[
            pltpu.VMEM((2, npages * page, hd), F32),
            pltpu.VMEM((2, npages * page, hd), F32),
            pltpu.VMEM((2, npages * HEAD_ROWS, page), F32),
            pltpu.SemaphoreType.DMA((2, 3)),
        ],
    )
    return pl.pallas_call(
        functools.partial(_fox_decode_body, npages=npages, page=page),
        out_shape=jax.ShapeDtypeStruct((nseq, HEAD_ROWS, hd), F32),
        grid_spec=grid_spec,
        compiler_params=_cparams(("arbitrary",)),
        name="fox_decode",
    )(pt, q8, kn, vn, lfn, gpre, tpre, kc, vc, lfc)


def _mla_decode_body(pt_ref, ql_ref, qr_ref, cn_ref, rn_ref, c_hbm, r_hbm, o_ref,
                     cbuf, rbuf, sems, *, npages, page):
    b = pl.program_id(0)
    nb = pl.num_programs(0)
    slot = b % 2

    def copies(seq, sl, fn):
        def body(pg, carry):
            pid = pt_ref[seq, pg]
            row = pl.multiple_of(pg * page, page)
            fn(pltpu.make_async_copy(c_hbm.at[pid], cbuf.at[sl, pl.ds(row, page), :], sems.at[sl, 0]))
            fn(pltpu.make_async_copy(r_hbm.at[pid], rbuf.at[sl, :, pl.ds(row, page)], sems.at[sl, 1]))
            return carry
        lax.fori_loop(0, npages, body, 0)

    @pl.when(b == 0)
    def _():
        copies(0, 0, lambda c: c.start())

    @pl.when(b + 1 < nb)
    def _():
        copies(b + 1, 1 - slot, lambda c: c.start())

    copies(b, slot, lambda c: c.wait())

    ql = ql_ref[0]
    qr = qr_ref[0]
    cb = cbuf[slot].astype(BF16)
    rb = rbuf[slot].astype(BF16)
    s = (_dot_nt(ql.astype(BF16), cb) + _dot(qr.astype(BF16), rb)) * MLA_SCALE
    s_new = (jnp.sum(ql * cn_ref[0], axis=-1, keepdims=True)
             + jnp.sum(qr * rn_ref[0], axis=-1, keepdims=True)) * MLA_SCALE
    m = jnp.maximum(jnp.max(s, axis=-1, keepdims=True), s_new)
    p = jnp.exp(s - m)
    p_new = jnp.exp(s_new - m)
    l = jnp.sum(p, axis=-1, keepdims=True) + p_new
    o = _dot(p.astype(BF16), cb) + p_new * cn_ref[0]
    o_ref[0] = o / l


def _mla_decode(pt, ql8, qr8, cn, rn, cc, rc, *, npages, page):
    nseq = ql8.shape[0]
    blk = lambda r, n: pl.BlockSpec((1, r, n), lambda b, pt_ref: (b, 0, 0))
    grid_spec = pltpu.PrefetchScalarGridSpec(
        num_scalar_prefetch=1,
        grid=(nseq,),
        in_specs=[blk(HEAD_ROWS, MLA_KV_RANK), blk(HEAD_ROWS, MLA_ROPE), blk(1, MLA_KV_RANK),
                  blk(1, MLA_ROPE),
                  pl.BlockSpec(memory_space=pl.ANY), pl.BlockSpec(memory_space=pl.ANY)],
        out_specs=blk(HEAD_ROWS, MLA_KV_RANK),
        scratch_shapes=[
            pltpu.VMEM((2, npages * page, MLA_KV_RANK), F32),
            pltpu.VMEM((2, MLA_ROPE, npages * page), F32),
            pltpu.SemaphoreType.DMA((2, 2)),
        ],
    )
    return pl.pallas_call(
        functools.partial(_mla_decode_body, npages=npages, page=page),
        out_shape=jax.ShapeDtypeStruct((nseq, HEAD_ROWS, MLA_KV_RANK), F32),
        grid_spec=grid_spec,
        compiler_params=_cparams(("arbitrary",)),
        name="mla_decode",
    )(pt, ql8, qr8, cn, rn, cc, rc)


def _final_norm_body(x_ref, g_ref, o_ref):
    o_ref[...] = _rms(x_ref[...], g_ref[...])


def _final_norm(x, g, *, tm):
    m, d = x.shape
    return pl.pallas_call(
        _final_norm_body,
        out_shape=jax.ShapeDtypeStruct((m, d), F32),
        grid=(m // tm,),
        in_specs=[pl.BlockSpec((tm, d), lambda i: (i, 0)), pl.BlockSpec((1, d), lambda i: (0, 0))],
        out_specs=pl.BlockSpec((tm, d), lambda i: (i, 0)),
        compiler_params=_cparams(("parallel",)),
        name="final_norm",
    )(x, g)


def _rot_cols(w):
    shp = w.shape
    w4 = w.reshape(shp[:-1] + (shp[-1] // MLA_ROPE, 2, MLA_ROPE // 2))
    return w4[..., ::-1, :].reshape(shp)


def _prep_layer(l, p, f_pad):
    d_ff = p["w_ffn1_gate"].shape[2]
    padf = lambda w: jnp.pad(w.astype(BF16), ((0, 0), (0, f_pad - d_ff)))
    padr = lambda w: jnp.pad(w.astype(BF16), ((0, f_pad - d_ff), (0, 0)))
    w = {}
    for k in ("1", "2"):
        w["g" + k] = padf(p[f"w_ffn{k}_gate"][l])
        w["u" + k] = padf(p[f"w_ffn{k}_up"][l])
        w["d" + k] = padr(p[f"w_ffn{k}_down"][l])
        w["n" + k] = p[f"norm_ffn{k}"][l][None, :]
    wi = p["w_in"][l]
    offs = np.cumsum((0, 512, 128, 128, 4, 512, 512, 512, 512, 256, 64, 512))
    fq, fk, fv, ff, cv, cb, cc, mq, mkv, mkr, su = [wi[:, offs[i]:offs[i + 1]] for i in range(11)]
    dm = wi.shape[0]
    w["w_in"] = jnp.concatenate(
        [fq, fk, fv, cv, cb, cc, mq, mkv, su, mkr, _rot_cols(mkr), ff, jnp.zeros((dm, LANE - 4), F32)],
        axis=1).astype(BF16)
    w["norm_mix"] = p["norm_mix"][l][None, :]
    w["bias_f"] = jnp.pad(p["fox_b_f"][l], (0, LANE - FOX_HEADS))[None, :]
    wuq = p["mla_w_uq"][l].reshape(MLA_Q_RANK, MLA_HEADS, MLA_NOPE + MLA_ROPE)
    w["wq_nope"] = wuq[:, :, :MLA_NOPE].reshape(MLA_Q_RANK, MLA_HEADS * MLA_NOPE).astype(BF16)
    wr = wuq[:, :, MLA_NOPE:].reshape(MLA_Q_RANK, MLA_HEADS * MLA_ROPE)
    w["wq_rope"] = wr.astype(BF16)
    w["wq_rope_rot"] = _rot_cols(wr).astype(BF16)
    w["w_uk"] = jnp.transpose(p["mla_w_uk"][l], (1, 2, 0)).astype(BF16)
    w["w_uv"] = jnp.transpose(p["mla_w_uv"][l], (1, 0, 2)).astype(BF16)
    w["g_q"] = p["mla_norm_q"][l][None, :]
    w["g_kv"] = p["mla_norm_kv"][l][None, :]
    lr, li = p["s5_lambda_re"][l], p["s5_lambda_im"][l]
    step = jnp.exp(p["s5_log_step"][l])[:, None]
    mag = jnp.exp(lr * step)
    lbr, lbi = mag * jnp.cos(li * step), mag * jnp.sin(li * step)
    den = lr * lr + li * li
    cr = ((lbr - 1.0) * lr + lbi * li) / den
    ci = (lbi * lr - (lbr - 1.0) * li) / den
    bbr = cr[..., None] * p["s5_b_re"][l] - ci[..., None] * p["s5_b_im"][l]
    bbi = cr[..., None] * p["s5_b_im"][l] + ci[..., None] * p["s5_b_re"][l]
    eye = jnp.eye(S5_GROUPS, dtype=F32)
    bmat = lambda t: jnp.einsum("gnc,gh->gchn", t, eye).reshape(GROUP_WIDTH, S5_LANES)
    cmat = lambda t: jnp.einsum("gcn,gh->gnhc", t, eye).reshape(S5_LANES, GROUP_WIDTH)
    bre, bim = bmat(bbr), bmat(bbi)
    w["bre"], w["bim"] = bre.astype(BF16), bim.astype(BF16)
    w["bre_lo"] = (bre - w["bre"].astype(F32)).astype(BF16)
    w["bim_lo"] = (bim - w["bim"].astype(F32)).astype(BF16)
    w["cre"] = cmat(p["s5_c_re"][l]).astype(BF16)
    w["cim"] = cmat(-p["s5_c_im"][l]).astype(BF16)
    pr, pi = [lbr.reshape(1, S5_LANES)], [lbi.reshape(1, S5_LANES)]
    for _ in range(15):
        r, i = pr[-1], pi[-1]
        pr.append(r * r - i * i)
        pi.append(2.0 * r * i)
    w["pow_re"], w["pow_im"] = jnp.concatenate(pr, axis=0), jnp.concatenate(pi, axis=0)
    w["s5_d"] = p["s5_d"][l][None, :]
    w["w_glu"] = p["s5_w_glu"][l].astype(BF16)
    w["gains"] = p["norm_group_out"][l].reshape(4, GROUP_WIDTH)
    w["conv_w"] = jnp.pad(p["conv_w"][l], ((0, SUBLANE - CONV_WIDTH), (0, 0)))
    w["w_out"] = p["w_out"][l].astype(BF16)
    return w


def _rope_tables(pos):
    half = MLA_ROPE // 2
    inv_freq = ROPE_THETA ** (-jnp.arange(half, dtype=F32) / half)
    ang = pos.astype(F32)[:, None] * inv_freq
    cos, sin = jnp.cos(ang), jnp.sin(ang)
    cos64 = jnp.concatenate([cos, cos], axis=1)
    sin64 = jnp.concatenate([-sin, sin], axis=1)
    return jnp.tile(cos64, (1, MLA_HEADS)), jnp.tile(sin64, (1, MLA_HEADS))


def _decode_consts(npages, page):
    r = np.arange(npages * HEAD_ROWS)
    c = np.arange(page)
    gpre = (c[:, None] <= c[None, :]).astype(np.float32)
    tpre = (((r % HEAD_ROWS)[:, None] == (r % HEAD_ROWS)[None, :])
            & ((r // HEAD_ROWS)[None, :] < (r // HEAD_ROWS)[:, None])).astype(np.float32)
    return jnp.asarray(gpre, BF16), jnp.asarray(tpre, BF16)


def _pick(m, pref):
    for t in pref:
        if m % t == 0:
            return t
    return m


def kernel(x_prompt, x_sample, cache_fox_k, cache_fox_v, cache_fox_logf, cache_mla_ckv, cache_mla_krope, state_conv, state_s5_re, state_s5_im, page_table, norm_ffn1, w_ffn1_gate, w_ffn1_up, w_ffn1_down, norm_mix, w_in, fox_b_f, conv_w, mla_norm_q, mla_w_uq, mla_norm_kv, mla_w_uk, mla_w_uv, s5_lambda_re, s5_lambda_im, s5_log_step, s5_b_re, s5_b_im, s5_c_re, s5_c_im, s5_d, s5_w_glu, norm_group_out, w_out, norm_ffn2, w_ffn2_gate, w_ffn2_up, w_ffn2_down, norm_final):
    p = dict(norm_ffn1=norm_ffn1, w_ffn1_gate=w_ffn1_gate, w_ffn1_up=w_ffn1_up,
             w_ffn1_down=w_ffn1_down, norm_mix=norm_mix, w_in=w_in, fox_b_f=fox_b_f, conv_w=conv_w,
             mla_norm_q=mla_norm_q, mla_w_uq=mla_w_uq, mla_norm_kv=mla_norm_kv, mla_w_uk=mla_w_uk,
             mla_w_uv=mla_w_uv, s5_lambda_re=s5_lambda_re, s5_lambda_im=s5_lambda_im,
             s5_log_step=s5_log_step, s5_b_re=s5_b_re, s5_b_im=s5_b_im, s5_c_re=s5_c_re,
             s5_c_im=s5_c_im, s5_d=s5_d, s5_w_glu=s5_w_glu, norm_group_out=norm_group_out,
             w_out=w_out, norm_ffn2=norm_ffn2, w_ffn2_gate=w_ffn2_gate, w_ffn2_up=w_ffn2_up,
             w_ffn2_down=w_ffn2_down)
    depth = w_in.shape[0]
    bp, sp, dm = x_prompt.shape
    bd, sd, _ = x_sample.shape
    assert sd == 1, "sample group is a single-token decode step"
    n_pool, page = cache_fox_k.shape[1], cache_fox_k.shape[2]
    npages = page_table.shape[1]
    n_past = npages * page
    mp = bp * sp
    d_ff = w_ffn1_gate.shape[2]
    tf = 512
    f_pad = -(-d_ff // tf) * tf

    tm_p = _pick(mp, (512, 256, 128))
    tm_d = bd
    tq = _pick(sp, (512, 256, 128))
    lt = _pick(sp, (128,))

    cos_p, sin_p = _rope_tables(jnp.arange(sp))
    cos_d, sin_d = _rope_tables(jnp.full((bd,), n_past))
    gpre, tpre = _decode_consts(npages, page)

    kc = cache_fox_k.reshape(depth * n_pool, page, FOX_HEAD_DIM)
    vc = cache_fox_v.reshape(depth * n_pool, page, FOX_HEAD_DIM)
    lfc = cache_fox_logf.transpose(0, 1, 3, 2).reshape(depth * n_pool, FOX_HEADS, page)
    ckc = cache_mla_ckv.reshape(depth * n_pool, page, MLA_KV_RANK)
    krc = cache_mla_krope.transpose(0, 1, 3, 2).reshape(depth * n_pool, MLA_ROPE, page)

    xp = x_prompt.reshape(mp, dm)
    xd = x_sample.reshape(bd, dm)
    acc = {k: [] for k in ("fox_k_p", "fox_v_p", "fox_logf_p", "mla_ckv_p", "mla_krope_p", "conv_p",
                           "s5_re_p", "s5_im_p", "fox_k_s", "fox_v_s", "fox_logf_s", "mla_ckv_s",
                           "mla_krope_s", "conv_s", "s5_re_s", "s5_im_s")}
    pad_heads = lambda t: jnp.pad(t, ((0, 0), (0, HEAD_ROWS - t.shape[1]), (0, 0)))

    for l in range(depth):
        w = _prep_layer(l, p, f_pad)
        xp = _ffn(xp, w["n1"], w["g1"], w["u1"], w["d1"], tm=tm_p, tf=tf)
        fq, fk, fv, cv, cb, cc, mq, mkv, su, kr, lf = _inproj(xp, w["norm_mix"], w["w_in"], w["bias_f"],
                                                              tm=_pick(mp, (256, 128)))
        logf = lf[:, :FOX_HEADS]
        cum = _cumsum_lanes(logf.reshape(bp, sp, FOX_HEADS).transpose(0, 2, 1).reshape(bp * FOX_HEADS, sp))
        ck = cum.reshape(bp, FOX_HEADS, sp)
        cq = ck.transpose(0, 2, 1).reshape(mp, FOX_HEADS)
        a_p = _fox_prompt(fq, fk, fv, cq, ck, nb=bp, seq=sp, tq=tq, tk=tq)
        ql, qr, ckv, ckvb, krope, kropeb = _mla_proj(
            mq, mkv, kr, cos_p, sin_p, w["g_q"], w["g_kv"], w["wq_nope"], w["wq_rope"],
            w["wq_rope_rot"], w["w_uk"], tm=_pick(sp, (256, 128)))
        c_p = _mla_prompt(ql, qr, ckvb, kropeb, w["w_uv"], nb=bp, seq=sp, tq=tq, tk=tq)
        d_p, sre_p, sim_p = _s5_prompt(su, w["bre"], w["bim"], w["cre"], w["cim"], w["pow_re"],
                                       w["pow_im"], w["s5_d"], w["w_glu"], nb=bp, seq=sp, lt=lt)
        xp, cs_p = _postmix_prompt(xp, a_p, cv, cb, cc, c_p, d_p, w["gains"], w["conv_w"], w["w_out"],
                                   seq=sp, tm=_pick(sp, (256, 128)))
        xp = _ffn(xp, w["n2"], w["g2"], w["u2"], w["d2"], tm=tm_p, tf=tf)
        acc["fox_k_p"].append(fk.reshape(bp, sp, 1, FOX_HEAD_DIM))
        acc["fox_v_p"].append(fv.reshape(bp, sp, 1, FOX_HEAD_DIM))
        acc["fox_logf_p"].append(logf.reshape(bp, sp, FOX_HEADS))
        acc["mla_ckv_p"].append(ckv.reshape(bp, sp, MLA_KV_RANK))
        acc["mla_krope_p"].append(krope.reshape(bp, sp, MLA_ROPE))
        acc["conv_p"].append(cs_p[:, SUBLANE - (CONV_WIDTH - 1):, :])
        acc["s5_re_p"].append(sre_p[:, SUBLANE - 1, :].reshape(bp, S5_GROUPS, S5_STATE))
        acc["s5_im_p"].append(sim_p[:, SUBLANE - 1, :].reshape(bp, S5_GROUPS, S5_STATE))

        xd = _ffn(xd, w["n1"], w["g1"], w["u1"], w["d1"], tm=tm_d, tf=tf)
        fq, fk, fv, cv, cb, cc, mq, mkv, su, kr, lf = _inproj(xd, w["norm_mix"], w["w_in"], w["bias_f"],
                                                              tm=tm_d)
        logf = lf[:, :FOX_HEADS]
        pt = page_table + l * n_pool
        q8 = pad_heads(fq.reshape(bd, FOX_HEADS, FOX_HEAD_DIM))
        lfn = jnp.broadcast_to(pad_heads(logf[:, :, None]), (bd, HEAD_ROWS, page))
        a8 = _fox_decode(pt, q8, fk[:, None, :], fv[:, None, :], lfn, gpre, tpre, kc, vc, lfc,
                         npages=npages, page=page)
        a_d = a8[:, :FOX_HEADS, :].reshape(bd, GROUP_WIDTH)
        ql, qr, ckv, ckvb, krope, kropeb = _mla_proj(
            mq, mkv, kr, cos_d, sin_d, w["g_q"], w["g_kv"], w["wq_nope"], w["wq_rope"],
            w["wq_rope_rot"], w["w_uk"], tm=tm_d)
        ql8 = pad_heads(ql.astype(F32).reshape(bd, MLA_HEADS, MLA_KV_RANK))
        qr8 = pad_heads(qr.astype(F32).reshape(bd, MLA_HEADS, MLA_ROPE))
        ol8 = _mla_decode(pt, ql8, qr8, ckv[:, None, :], krope[:, None, :], ckc, krc,
                          npages=npages, page=page)
        olat = ol8[:, :MLA_HEADS, :].reshape(bd, MLA_HEADS * MLA_KV_RANK)
        sc = state_conv[l]
        xd, xc_d, sre_d, sim_d = _postmix_sample(
            xd, a_d, cv, cb, cc, sc[:, 0, :], sc[:, 1, :], olat, su,
            state_s5_re[l].reshape(bd, S5_LANES), state_s5_im[l].reshape(bd, S5_LANES),
            w["gains"], w["conv_w"], w["w_out"], w["w_uv"], w["bre"], w["bre_lo"], w["bim"], w["bim_lo"],
            w["cre"], w["cim"], w["pow_re"][0:1], w["pow_im"][0:1], w["s5_d"], w["w_glu"])
        xd = _ffn(xd, w["n2"], w["g2"], w["u2"], w["d2"], tm=tm_d, tf=tf)
        acc["fox_k_s"].append(fk.reshape(bd, 1, 1, FOX_HEAD_DIM))
        acc["fox_v_s"].append(fv.reshape(bd, 1, 1, FOX_HEAD_DIM))
        acc["fox_logf_s"].append(logf.reshape(bd, 1, FOX_HEADS))
        acc["mla_ckv_s"].append(ckv.reshape(bd, 1, MLA_KV_RANK))
        acc["mla_krope_s"].append(krope.reshape(bd, 1, MLA_ROPE))
        acc["conv_s"].append(jnp.stack([sc[:, 1, :], xc_d], axis=1))
        acc["s5_re_s"].append(sre_d.reshape(bd, S5_GROUPS, S5_STATE))
        acc["s5_im_s"].append(sim_d.reshape(bd, S5_GROUPS, S5_STATE))

    gfin = norm_final[None, :]
    y_prompt = _final_norm(xp, gfin, tm=tm_p).reshape(bp, sp, dm)
    y_sample = _final_norm(xd, gfin, tm=tm_d).reshape(bd, sd, dm)
    st = {k: jnp.stack(v) for k, v in acc.items()}
    return (y_prompt, y_sample,
            st["fox_k_p"], st["fox_v_p"], st["fox_logf_p"], st["mla_ckv_p"], st["mla_krope_p"],
            st["conv_p"], st["s5_re_p"], st["s5_im_p"],
            st["fox_k_s"], st["fox_v_s"], st["fox_logf_s"], st["mla_ckv_s"], st["mla_krope_s"],
            st["conv_s"], st["s5_re_s"], st["s5_im_s"])
```

```python
import functools
import math

import jax
import jax.numpy as jnp
import numpy as np
from jax import lax
from jax.experimental import pallas as pl
from jax.experimental.pallas import tpu as pltpu

F32 = jnp.float32
BF16 = jnp.bfloat16

EPS = 1e-6
FOX_HEADS = 4
FOX_HEAD_DIM = 128
FOX_SCALE = FOX_HEAD_DIM ** -0.5
GROUP_WIDTH = 512
MLA_HEADS = 4
MLA_NOPE = 128
MLA_ROPE = 64
MLA_V = 128
MLA_Q_RANK = 512
MLA_KV_RANK = 256
MLA_SCALE = (MLA_NOPE + MLA_ROPE) ** -0.5
ROPE_THETA = 10000.0
S5_GROUPS = 32
S5_GROUP_SIZE = 16
S5_STATE = 64
S5_LANES = S5_GROUPS * S5_STATE
CONV_WIDTH = 3
LANE = 128
SUBLANE = 8
HEAD_ROWS = 8
VMEM_LIMIT = 56 * 1024 * 1024

_SEG = dict(fq=(0, 512), fkv=(512, 256), cv=(768, 512), cb=(1280, 512), cc=(1792, 512),
            mq=(2304, 512), mkv=(2816, 256), su=(3072, 512), kr=(3584, 128), ff=(3712, 128))
D_IN_PAD = 3840


def _cparams(sem, vmem=VMEM_LIMIT):
    return pltpu.CompilerParams(dimension_semantics=sem, vmem_limit_bytes=vmem)


def _rms(x, g):
    ms = jnp.mean(x * x, axis=-1, keepdims=True)
    return x * lax.rsqrt(ms + EPS) * g


def _dot(a, b):
    return jnp.dot(a, b, preferred_element_type=F32)


def _dot_nt(a, b):
    return lax.dot_general(a, b, (((1,), (1,)), ((), ())), preferred_element_type=F32)


def _const_spec(shape):
    nd = len(shape)
    return pl.BlockSpec(shape, lambda *_: (0,) * nd, pipeline_mode=pl.Buffered(1))


def _ffn_body(x_ref, g_ref, wg_ref, wu_ref, wd_ref, o_ref, h_ref, *, tf, tail):
    f = pl.program_id(1)
    nf = pl.num_programs(1)

    @pl.when(f == 0)
    def _():
        x = x_ref[...]
        h_ref[...] = _rms(x, g_ref[...]).astype(BF16)
        o_ref[...] = x

    def accumulate(n):
        h = h_ref[...]
        a = _dot(h, wg_ref[:, :n])
        b = _dot(h, wu_ref[:, :n])
        act = (a * jax.nn.sigmoid(a) * b).astype(BF16)
        o_ref[...] += 0.5 * _dot(act, wd_ref[:n, :])

    if tail == tf:
        accumulate(tf)
    else:
        pl.when(f < nf - 1)(lambda: accumulate(tf))
        pl.when(f == nf - 1)(lambda: accumulate(tail))


def _ffn(x, g, wg, wu, wd, l, *, tm, tf):
    m, d = x.shape
    d_ff = wg.shape[2]
    nf = pl.cdiv(d_ff, tf)
    tail = d_ff - (nf - 1) * tf
    assert tail % LANE == 0
    return pl.pallas_call(
        functools.partial(_ffn_body, tf=tf, tail=tail),
        out_shape=jax.ShapeDtypeStruct((m, d), F32),
        grid=(m // tm, nf),
        in_specs=[
            pl.BlockSpec((tm, d), lambda i, f: (i, 0)),
            pl.BlockSpec((1, d), lambda i, f: (0, 0)),
            pl.BlockSpec((None, d, tf), lambda i, f: (l, 0, f)),
            pl.BlockSpec((None, d, tf), lambda i, f: (l, 0, f)),
            pl.BlockSpec((None, tf, d), lambda i, f: (l, f, 0)),
        ],
        out_specs=pl.BlockSpec((tm, d), lambda i, f: (i, 0)),
        scratch_shapes=[pltpu.VMEM((tm, d), BF16)],
        compiler_params=_cparams(("parallel", "arbitrary")),
        name="ffn",
    )(x, g, wg, wu, wd)


def _inproj_body(x_ref, g_ref, w_ref, bias_ref, fq_ref, fk_ref, fv_ref, cv_ref, cb_ref, cc_ref,
                 mq_ref, mkv_ref, su_ref, kr_ref, lf_ref):
    h = _rms(x_ref[...], g_ref[...]).astype(BF16)

    def seg(name):
        a, n = _SEG[name]
        return _dot_nt(h, w_ref[a:a + n, :])

    fq_ref[...] = seg("fq")
    fkv = seg("fkv")
    fk_ref[...] = fkv[:, :FOX_HEAD_DIM]
    fv_ref[...] = fkv[:, FOX_HEAD_DIM:]
    cv_ref[...] = seg("cv")
    cb_ref[...] = seg("cb")
    cc_ref[...] = seg("cc")
    mq_ref[...] = seg("mq")
    mkv_ref[...] = seg("mkv")
    su_ref[...] = seg("su")
    kr_ref[...] = seg("kr")
    zf = seg("ff") + bias_ref[...]
    lf_ref[...] = -(jnp.maximum(-zf, 0.0) + jnp.log1p(jnp.exp(-jnp.abs(zf))))


def _inproj(x, g, wt, bias, l, *, tm):
    m, d = x.shape
    widths = [512, 128, 128, 512, 512, 512, 512, 256, 512, 128, 128]
    return pl.pallas_call(
        _inproj_body,
        out_shape=[jax.ShapeDtypeStruct((m, n), F32) for n in widths],
        grid=(m // tm,),
        in_specs=[
            pl.BlockSpec((tm, d), lambda i: (i, 0)),
            pl.BlockSpec((1, d), lambda i: (0, 0)),
            pl.BlockSpec((None,) + wt.shape[1:], lambda i: (l, 0, 0), pipeline_mode=pl.Buffered(1)),
            pl.BlockSpec((1, LANE), lambda i: (0, 0)),
        ],
        out_specs=[pl.BlockSpec((tm, n), lambda i: (i, 0)) for n in widths],
        compiler_params=_cparams(("parallel",)),
        name="inproj",
    )(x, g, wt, bias)


def _cumsum_body(x_ref, o_ref):
    x = x_ref[...]
    n = x.shape[1]
    lane = lax.broadcasted_iota(jnp.int32, x.shape, 1)
    d = 1
    while d < n:
        x = x + jnp.where(lane >= d, pltpu.roll(x, d, axis=1), 0.0)
        d *= 2
    o_ref[...] = x


def _cumsum_lanes(x):
    return pl.pallas_call(
        _cumsum_body, out_shape=jax.ShapeDtypeStruct(x.shape, F32), name="logf_cumsum")(x)


def _lanes(x, reps):
    return x if reps == 1 else jnp.concatenate([x] * reps, axis=1)


def _flash_init(m_ref, l_ref, acc_ref):
    m_ref[...] = jnp.full(m_ref.shape, -jnp.inf, F32)
    l_ref[...] = jnp.zeros(l_ref.shape, F32)
    acc_ref[...] = jnp.zeros(acc_ref.shape, F32)


def _flash_update(s, vb, m_ref, l_ref, acc_ref, h, cols):
    tk = s.shape[1]
    m_prev = m_ref[h]
    m_new = jnp.maximum(m_prev, jnp.max(s, axis=-1, keepdims=True))
    alpha = jnp.exp(m_prev - m_new)
    p = jnp.exp(s - _lanes(m_new, tk // LANE))
    psum = p[:, :LANE]
    for c in range(1, tk // LANE):
        psum = psum + p[:, c * LANE:(c + 1) * LANE]
    l_ref[h] = alpha * l_ref[h] + psum
    width = cols.stop - cols.start
    acc_ref[:, cols] = _lanes(alpha, width // LANE) * acc_ref[:, cols] + _dot(p.astype(BF16), vb)
    m_ref[h] = m_new


def _causal_mask(tq, tk):
    return (lax.broadcasted_iota(jnp.int32, (tq, tk), 1) <= lax.broadcasted_iota(jnp.int32, (tq, tk), 0))


def _fox_prompt_body(q_ref, k_ref, v_ref, cq_ref, ck_ref, o_ref, m_ref, l_ref, acc_ref, *, tq, tk):
    i = pl.program_id(1)
    j = pl.program_id(2)

    @pl.when(j == 0)
    def _():
        _flash_init(m_ref, l_ref, acc_ref)

    def step(diagonal):
        kb = k_ref[...].astype(BF16)
        vb = v_ref[...].astype(BF16)
        for h in range(FOX_HEADS):
            hs = slice(h * FOX_HEAD_DIM, (h + 1) * FOX_HEAD_DIM)
            s = _dot_nt(q_ref[:, hs].astype(BF16), kb) * FOX_SCALE
            s = s + _lanes(cq_ref[:, hs], tk // LANE) - ck_ref[0, h:h + 1, :]
            if diagonal:
                s = jnp.where(_causal_mask(tq, tk), s, -jnp.inf)
            _flash_update(s, vb, m_ref, l_ref, acc_ref, h, hs)

    pl.when(j < i)(lambda: step(False))
    pl.when(j == i)(lambda: step(True))

    @pl.when(j == pl.num_programs(2) - 1)
    def _():
        for h in range(FOX_HEADS):
            hs = slice(h * FOX_HEAD_DIM, (h + 1) * FOX_HEAD_DIM)
            o_ref[:, hs] = acc_ref[:, hs] / jnp.sum(l_ref[h], axis=-1, keepdims=True)


def _fox_prompt(q, k, v, cq, ck, *, nb, seq, tq, tk):
    assert tq == tk
    nq, nk = seq // tq, seq // tk
    kv_idx = lambda b, i, j: (b * nk + jnp.minimum(j, i), 0)
    return pl.pallas_call(
        functools.partial(_fox_prompt_body, tq=tq, tk=tk),
        out_shape=jax.ShapeDtypeStruct(q.shape, F32),
        grid=(nb, nq, nk),
        in_specs=[
            pl.BlockSpec((tq, q.shape[1]), lambda b, i, j: (b * nq + i, 0)),
            pl.BlockSpec((tk, FOX_HEAD_DIM), kv_idx),
            pl.BlockSpec((tk, FOX_HEAD_DIM), kv_idx),
            pl.BlockSpec((tq, cq.shape[1]), lambda b, i, j: (b * nq + i, 0)),
            pl.BlockSpec((1, FOX_HEADS, tk), lambda b, i, j: (b, 0, jnp.minimum(j, i))),
        ],
        out_specs=pl.BlockSpec((tq, q.shape[1]), lambda b, i, j: (b * nq + i, 0)),
        scratch_shapes=[
            pltpu.VMEM((FOX_HEADS, tq, LANE), F32),
            pltpu.VMEM((FOX_HEADS, tq, LANE), F32),
            pltpu.VMEM((tq, q.shape[1]), F32),
        ],
        compiler_params=_cparams(("parallel", "parallel", "arbitrary")),
        name="fox_prompt",
    )(q, k, v, cq, ck)


def _mla_proj_body(mq_ref, mkv_ref, kr_ref, cos_ref, sin_ref, gq_ref, gkv_ref, wn_ref, wr_ref,
                   wrr_ref, wuk_ref, ql_ref, qr_ref, ckv_ref, ckvb_ref, krope_ref, kropeb_ref):
    cq = _rms(mq_ref[...], gq_ref[...]).astype(BF16)
    cos = cos_ref[...]
    sin = sin_ref[...]
    qr_ref[...] = (_dot(cq, wr_ref[...]) * cos + _dot(cq, wrr_ref[...]) * sin).astype(BF16)
    qn = _dot(cq, wn_ref[...])
    for h in range(MLA_HEADS):
        qh = qn[:, h * MLA_NOPE:(h + 1) * MLA_NOPE].astype(BF16)
        ql_ref[:, h * MLA_KV_RANK:(h + 1) * MLA_KV_RANK] = _dot(qh, wuk_ref[h]).astype(BF16)
    ckv = _rms(mkv_ref[...], gkv_ref[...])
    ckv_ref[...] = ckv
    ckvb_ref[...] = ckv.astype(BF16)
    kr = kr_ref[...]
    krope = kr[:, :MLA_ROPE] * cos[:, :MLA_ROPE] + kr[:, MLA_ROPE:] * sin[:, :MLA_ROPE]
    krope_ref[...] = krope
    kropeb_ref[...] = krope.astype(BF16)


def _mla_proj(mq, mkv, kr, cos, sin, gq, gkv, wn, wr, wrr, wuk, *, tm):
    m = mq.shape[0]
    ntab = cos.shape[0] // tm
    row = lambda n: pl.BlockSpec((tm, n), lambda i: (i, 0))
    tab = pl.BlockSpec((tm, cos.shape[1]), lambda i: (i % ntab, 0))
    return pl.pallas_call(
        _mla_proj_body,
        out_shape=[
            jax.ShapeDtypeStruct((m, MLA_HEADS * MLA_KV_RANK), BF16),
            jax.ShapeDtypeStruct((m, MLA_HEADS * MLA_ROPE), BF16),
            jax.ShapeDtypeStruct((m, MLA_KV_RANK), F32),
            jax.ShapeDtypeStruct((m, MLA_KV_RANK), BF16),
            jax.ShapeDtypeStruct((m, MLA_ROPE), F32),
            jax.ShapeDtypeStruct((m, MLA_ROPE), BF16),
        ],
        grid=(m // tm,),
        in_specs=[row(MLA_Q_RANK), row(MLA_KV_RANK), row(LANE), tab, tab,
                  pl.BlockSpec((1, MLA_Q_RANK), lambda i: (0, 0)),
                  pl.BlockSpec((1, MLA_KV_RANK), lambda i: (0, 0)),
                  _const_spec(wn.shape), _const_spec(wr.shape), _const_spec(wrr.shape),
                  _const_spec(wuk.shape)],
        out_specs=[row(MLA_HEADS * MLA_KV_RANK), row(MLA_HEADS * MLA_ROPE), row(MLA_KV_RANK),
                   row(MLA_KV_RANK), row(MLA_ROPE), row(MLA_ROPE)],
        compiler_params=_cparams(("parallel",)),
        name="mla_proj",
    )(mq, mkv, kr, cos, sin, gq, gkv, wn, wr, wrr, wuk)


def _mla_prompt_body(ql_ref, qr_ref, ckv_ref, kr_ref, wuv_ref, o_ref, m_ref, l_ref, acc_ref,
                     *, tq, tk):
    i = pl.program_id(1)
    j = pl.program_id(2)

    @pl.when(j == 0)
    def _():
        _flash_init(m_ref, l_ref, acc_ref)

    def step(diagonal):
        cb = ckv_ref[...]
        kb = kr_ref[...]
        for h in range(MLA_HEADS):
            ls = slice(h * MLA_KV_RANK, (h + 1) * MLA_KV_RANK)
            rs = slice(h * MLA_ROPE, (h + 1) * MLA_ROPE)
            s = (_dot_nt(ql_ref[:, ls], cb) + _dot_nt(qr_ref[:, rs], kb)) * MLA_SCALE
            if diagonal:
                s = jnp.where(_causal_mask(tq, tk), s, -jnp.inf)
            _flash_update(s, cb, m_ref, l_ref, acc_ref, h, ls)

    pl.when(j < i)(lambda: step(False))
    pl.when(j == i)(lambda: step(True))

    @pl.when(j == pl.num_programs(2) - 1)
    def _():
        for h in range(MLA_HEADS):
            ls = slice(h * MLA_KV_RANK, (h + 1) * MLA_KV_RANK)
            o_lat = (acc_ref[:, ls] / jnp.sum(l_ref[h], axis=-1, keepdims=True)).astype(BF16)
            o_ref[:, h * MLA_V:(h + 1) * MLA_V] = _dot(o_lat, wuv_ref[h])


def _mla_prompt(ql, qr, ckvb, krb, wuv, *, nb, seq, tq, tk):
    assert tq == tk
    nq, nk = seq // tq, seq // tk
    kv_idx = lambda b, i, j: (b * nk + jnp.minimum(j, i), 0)
    m = ql.shape[0]
    return pl.pallas_call(
        functools.partial(_mla_prompt_body, tq=tq, tk=tk),
        out_shape=jax.ShapeDtypeStruct((m, MLA_HEADS * MLA_V), F32),
        grid=(nb, nq, nk),
        in_specs=[
            pl.BlockSpec((tq, ql.shape[1]), lambda b, i, j: (b * nq + i, 0)),
            pl.BlockSpec((tq, qr.shape[1]), lambda b, i, j: (b * nq + i, 0)),
            pl.BlockSpec((tk, MLA_KV_RANK), kv_idx),
            pl.BlockSpec((tk, MLA_ROPE), kv_idx),
            _const_spec(wuv.shape),
        ],
        out_specs=pl.BlockSpec((tq, MLA_HEADS * MLA_V), lambda b, i, j: (b * nq + i, 0)),
        scratch_shapes=[
            pltpu.VMEM((MLA_HEADS, tq, LANE), F32),
            pltpu.VMEM((MLA_HEADS, tq, LANE), F32),
            pltpu.VMEM((tq, MLA_HEADS * MLA_KV_RANK), F32),
        ],
        compiler_params=_cparams(("parallel", "parallel", "arbitrary")),
        name="mla_prompt",
    )(ql, qr, ckvb, krb, wuv)


def _gelu_tanh(y):
    return 0.5 * y * (1.0 + jnp.tanh(math.sqrt(2.0 / math.pi) * (y + 0.044715 * (y * y * y))))


def _s5_prompt_body(u_ref, bre_ref, bim_ref, cre_ref, cim_ref, pre_ref, pim_ref, d_ref, wg_ref,
                    o_ref, sre_ref, sim_ref, xre_ref, xim_ref, car_re, car_im, *, lt):
    t = pl.program_id(1)
    u = u_ref[...]
    ub = u.astype(BF16)
    xre_ref[...] = _dot(ub, bre_ref[...])
    xim_ref[...] = _dot(ub, bim_ref[...])

    @pl.when(t == 0)
    def _():
        car_re[...] = jnp.zeros(car_re.shape, F32)
        car_im[...] = jnp.zeros(car_im.shape, F32)

    first = lax.broadcasted_iota(jnp.int32, (SUBLANE, S5_LANES), 0) == 0
    lre = pre_ref[0:1, :]
    lim = pim_ref[0:1, :]
    cre = car_re[...]
    cim = car_im[...]
    xre_ref[0:SUBLANE, :] += jnp.where(first, lre * cre - lim * cim, 0.0)
    xim_ref[0:SUBLANE, :] += jnp.where(first, lre * cim + lim * cre, 0.0)

    nsteps = int(math.log2(lt))
    rowid = lax.broadcasted_iota(jnp.int32, (lt, LANE), 0)

    def lane_block(c, carry):
        off = pl.multiple_of(c * LANE, LANE)
        xr = xre_ref[:, pl.ds(off, LANE)]
        xi = xim_ref[:, pl.ds(off, LANE)]
        for k in range(nsteps):
            d = 1 << k
            ar = pre_ref[pl.ds(k, 1), pl.ds(off, LANE)]
            ai = pim_ref[pl.ds(k, 1), pl.ds(off, LANE)]
            keep = rowid >= d
            sr = jnp.where(keep, pltpu.roll(xr, d, axis=0), 0.0)
            si = jnp.where(keep, pltpu.roll(xi, d, axis=0), 0.0)
            xr, xi = xr + (ar * sr - ai * si), xi + (ar * si + ai * sr)
        xre_ref[:, pl.ds(off, LANE)] = xr
        xim_ref[:, pl.ds(off, LANE)] = xi
        return carry

    lax.fori_loop(0, S5_LANES // LANE, lane_block, 0)

    last_re = xre_ref[lt - SUBLANE:lt, :]
    last_im = xim_ref[lt - SUBLANE:lt, :]
    car_re[...] = jnp.broadcast_to(last_re[SUBLANE - 1:SUBLANE, :], car_re.shape)
    car_im[...] = jnp.broadcast_to(last_im[SUBLANE - 1:SUBLANE, :], car_im.shape)
    sre_ref[0] = last_re
    sim_ref[0] = last_im

    y = (_dot(xre_ref[...].astype(BF16), cre_ref[...]) + _dot(xim_ref[...].astype(BF16), cim_ref[...])
         + d_ref[...] * u)
    y = _gelu_tanh(y)
    o_ref[...] = y * jax.nn.sigmoid(_dot(y.astype(BF16), wg_ref[...]))


def _s5_prompt(u, bre, bim, cre, cim, pre, pim, dvec, wglu, *, nb, seq, lt):
    m = u.shape[0]
    nt = seq // lt
    return pl.pallas_call(
        functools.partial(_s5_prompt_body, lt=lt),
        out_shape=[
            jax.ShapeDtypeStruct((m, GROUP_WIDTH), F32),
            jax.ShapeDtypeStruct((nb, SUBLANE, S5_LANES), F32),
            jax.ShapeDtypeStruct((nb, SUBLANE, S5_LANES), F32),
        ],
        grid=(nb, nt),
        in_specs=[
            pl.BlockSpec((lt, GROUP_WIDTH), lambda b, t: (b * nt + t, 0)),
            _const_spec(bre.shape), _const_spec(bim.shape),
            _const_spec(cre.shape), _const_spec(cim.shape),
            _const_spec(pre.shape), _const_spec(pim.shape),
            pl.BlockSpec((1, GROUP_WIDTH), lambda b, t: (0, 0)),
            _const_spec(wglu.shape),
        ],
        out_specs=[
            pl.BlockSpec((lt, GROUP_WIDTH), lambda b, t: (b * nt + t, 0)),
            pl.BlockSpec((1, SUBLANE, S5_LANES), lambda b, t: (b, 0, 0)),
            pl.BlockSpec((1, SUBLANE, S5_LANES), lambda b, t: (b, 0, 0)),
        ],
        scratch_shapes=[
            pltpu.VMEM((lt, S5_LANES), F32), pltpu.VMEM((lt, S5_LANES), F32),
            pltpu.VMEM((SUBLANE, S5_LANES), F32), pltpu.VMEM((SUBLANE, S5_LANES), F32),
        ],
        compiler_params=_cparams(("parallel", "arbitrary")),
        name="s5_prompt",
    )(u, bre, bim, cre, cim, pre, pim, dvec, wglu)


def _group_out(x, outs, gains_ref, wout_ref):
    y = x
    for g, o in enumerate(outs):
        n = _rms(o, gains_ref[g:g + 1, :]).astype(BF16)
        y = y + _dot(n, wout_ref[g * GROUP_WIDTH:(g + 1) * GROUP_WIDTH, :])
    return y


def _postmix_prompt_body(x_ref, a_ref, cv_ref, cb_ref, cc_ref, hv_ref, hc_ref, c_ref, d_ref,
                         gains_ref, cw_ref, wout_ref, o_ref, cs_ref, *, tm, tiles_per_seq):
    i = pl.program_id(0)
    xc = cc_ref[...] * cv_ref[...]
    halo = hc_ref[...] * hv_ref[...]
    halo = jnp.where(i % tiles_per_seq == 0, 0.0, halo)
    rowid = lax.broadcasted_iota(jnp.int32, xc.shape, 0)
    h7 = halo[SUBLANE - 1:SUBLANE, :]
    h6 = halo[SUBLANE - 2:SUBLANE - 1, :]
    prev1 = jnp.where(rowid == 0, h7, pltpu.roll(xc, 1, axis=0))
    prev2 = jnp.where(rowid == 0, h6, jnp.where(rowid == 1, h7, pltpu.roll(xc, 2, axis=0)))
    conv = cw_ref[0:1, :] * prev2 + cw_ref[1:2, :] * prev1 + cw_ref[2:3, :] * xc
    b = cb_ref[...] * conv
    cs_ref[0] = xc[tm - SUBLANE:tm, :]
    o_ref[...] = _group_out(x_ref[...], [a_ref[...], b, c_ref[...], d_ref[...]], gains_ref, wout_ref)


def _postmix_prompt(x, a, cv, cb, cc, c, d, gains, cw, wout, l, *, seq, tm):
    m, dm = x.shape
    tiles_per_seq = seq // tm
    nb = m // seq
    row = lambda n: pl.BlockSpec((tm, n), lambda i: (i, 0))
    halo = pl.BlockSpec((SUBLANE, GROUP_WIDTH),
                        lambda i: (jnp.maximum(i * (tm // SUBLANE) - 1, 0), 0))
    return pl.pallas_call(
        functools.partial(_postmix_prompt_body, tm=tm, tiles_per_seq=tiles_per_seq),
        out_shape=[jax.ShapeDtypeStruct((m, dm), F32),
                   jax.ShapeDtypeStruct((nb, SUBLANE, GROUP_WIDTH), F32)],
        grid=(m // tm,),
        in_specs=[row(dm), row(GROUP_WIDTH), row(GROUP_WIDTH), row(GROUP_WIDTH), row(GROUP_WIDTH),
                  halo, halo, row(GROUP_WIDTH), row(GROUP_WIDTH),
                  pl.BlockSpec(gains.shape, lambda i: (0, 0)),
                  pl.BlockSpec(cw.shape, lambda i: (0, 0)),
                  pl.BlockSpec((None,) + wout.shape[1:], lambda i: (l, 0, 0),
                               pipeline_mode=pl.Buffered(1))],
        out_specs=[row(dm),
                   pl.BlockSpec((1, SUBLANE, GROUP_WIDTH), lambda i: (i // tiles_per_seq, 0, 0))],
        compiler_params=_cparams(("arbitrary",)),
        name="postmix_prompt",
    )(x, a, cv, cb, cc, cv, cc, c, d, gains, cw, wout)


def _split3(x):
    hi = x.astype(BF16)
    r1 = x - hi.astype(F32)
    mid = r1.astype(BF16)
    lo = (r1 - mid.astype(F32)).astype(BF16)
    return hi, mid, lo


def _postmix_sample_body(x_ref, a_ref, cv_ref, cb_ref, cc_ref, s0_ref, s1_ref, olat_ref, u_ref,
                         x0re_ref, x0im_ref, gains_ref, cw_ref, wout_ref, wuv_ref,
                         bre_hi_ref, bre_lo_ref, bim_hi_ref, bim_lo_ref, cre_ref, cim_ref,
                         lre_ref, lim_ref, d_ref, wg_ref,
                         o_ref, xc_ref, sre_ref, sim_ref):
    xc = cc_ref[...] * cv_ref[...]
    conv = cw_ref[0:1, :] * s0_ref[...] + cw_ref[1:2, :] * s1_ref[...] + cw_ref[2:3, :] * xc
    b = cb_ref[...] * conv
    xc_ref[...] = xc
    cs = []
    for h in range(MLA_HEADS):
        ol = olat_ref[:, h * MLA_KV_RANK:(h + 1) * MLA_KV_RANK].astype(BF16)
        cs.append(_dot(ol, wuv_ref[h]))
    c = jnp.concatenate(cs, axis=1)
    u = u_ref[...]
    u_hi, u_mid, _ = _split3(u)
    bu_re = (_dot(u_hi, bre_hi_ref[...]) + _dot(u_mid, bre_hi_ref[...]) + _dot(u_hi, bre_lo_ref[...]))
    bu_im = (_dot(u_hi, bim_hi_ref[...]) + _dot(u_mid, bim_hi_ref[...]) + _dot(u_hi, bim_lo_ref[...]))
    lre = lre_ref[...]
    lim = lim_ref[...]
    x0re = x0re_ref[...]
    x0im = x0im_ref[...]
    nre = lre * x0re - lim * x0im + bu_re
    nim = lre * x0im + lim * x0re + bu_im
    sre_ref[...] = nre
    sim_ref[...] = nim
    y = _dot(nre.astype(BF16), cre_ref[...]) + _dot(nim.astype(BF16), cim_ref[...]) + d_ref[...] * u
    y = _gelu_tanh(y)
    d = y * jax.nn.sigmoid(_dot(y.astype(BF16), wg_ref[...]))
    o_ref[...] = _group_out(x_ref[...], [a_ref[...], b, c, d], gains_ref, wout_ref)


def _postmix_sample(x, a, cv, cb, cc, s0, s1, olat, u, x0re, x0im, gains, cw, wout, wuv,
                    bre_hi, bre_lo, bim_hi, bim_lo, cre, cim, lre, lim, dvec, wglu, l):
    m, dm = x.shape
    args = (x, a, cv, cb, cc, s0, s1, olat, u, x0re, x0im, gains, cw, wout, wuv,
            bre_hi, bre_lo, bim_hi, bim_lo, cre, cim, lre, lim, dvec, wglu)
    whole = lambda t: pl.BlockSpec(t.shape, lambda i: (0,) * t.ndim, pipeline_mode=pl.Buffered(1))
    in_specs = [whole(t) for t in args]
    in_specs[13] = pl.BlockSpec((None,) + wout.shape[1:], lambda i: (l, 0, 0),
                                pipeline_mode=pl.Buffered(1))
    out_shape = [jax.ShapeDtypeStruct((m, dm), F32),
                 jax.ShapeDtypeStruct((m, GROUP_WIDTH), F32),
                 jax.ShapeDtypeStruct((m, S5_LANES), F32),
                 jax.ShapeDtypeStruct((m, S5_LANES), F32)]
    return pl.pallas_call(
        _postmix_sample_body,
        out_shape=out_shape,
        grid=(1,),
        in_specs=in_specs,
        out_specs=[pl.BlockSpec(s.shape, lambda i: (0, 0)) for s in out_shape],
        compiler_params=_cparams(("arbitrary",)),
        name="postmix_sample",
    )(*args)


DECODE_AHEAD = 2
DECODE_SLOTS = DECODE_AHEAD + 1


def _ring_prefetch(b, nb, fetch):
    @pl.when(b == 0)
    def _():
        for a in range(DECODE_AHEAD):
            pl.when(a < nb)(functools.partial(fetch, a, a))

    @pl.when(b + DECODE_AHEAD < nb)
    def _():
        fetch(b + DECODE_AHEAD, (b + DECODE_AHEAD) % DECODE_SLOTS)


def _wait_all(buf_view, sem):
    pltpu.make_async_copy(buf_view, buf_view, sem).wait()


def _fox_decode_body(pt_ref, q_ref, kn_ref, vn_ref, lfn_ref, gpre_ref, tpre_ref,
                     k_hbm, v_hbm, lf_hbm, o_ref, kbuf, vbuf, lfbuf, sems, *, npages, page):
    b = pl.program_id(0)
    nb = pl.num_programs(0)
    slot = b % DECODE_SLOTS

    def fetch(seq, sl):
        def body(pg, carry):
            pid = pt_ref[seq, pg]
            row = pl.multiple_of(pg * page, page)
            hrow = pl.multiple_of(pg * HEAD_ROWS, HEAD_ROWS)
            pltpu.make_async_copy(k_hbm.at[pid], kbuf.at[sl, pl.ds(row, page), :], sems.at[sl, 0]).start()
            pltpu.make_async_copy(v_hbm.at[pid], vbuf.at[sl, pl.ds(row, page), :], sems.at[sl, 1]).start()
            pltpu.make_async_copy(lf_hbm.at[pid], lfbuf.at[sl, pl.ds(hrow, FOX_HEADS), :],
                                  sems.at[sl, 2]).start()
            return carry
        lax.fori_loop(0, npages, body, 0)

    @pl.when(b == 0)
    def _():
        lfbuf[...] = jnp.zeros(lfbuf.shape, F32)

    _ring_prefetch(b, nb, fetch)
    _wait_all(kbuf.at[slot], sems.at[slot, 0])
    _wait_all(vbuf.at[slot], sems.at[slot, 1])
    _wait_all(lfbuf.at[slot, pl.ds(0, npages * FOX_HEADS), :], sems.at[slot, 2])

    rows = npages * HEAD_ROWS
    gpre = gpre_ref[...]
    wp = sum(_dot(part, gpre) for part in _split3(lfbuf[slot]))
    tot = jnp.broadcast_to(wp[:, page - 1:page], (rows, page))
    tpre = tpre_ref[...]
    off = sum(_dot(tpre, part) for part in _split3(tot))
    cum = wp + off
    total = cum[rows - HEAD_ROWS:rows, page - 1:page]
    bias0 = lfn_ref[0] + total

    q8 = q_ref[0]
    qb = q8.astype(BF16)
    kb = kbuf[slot].astype(BF16)
    s_all = _dot_nt(qb, kb) * FOX_SCALE
    s_new = jnp.sum(q8 * kn_ref[0], axis=-1, keepdims=True) * FOX_SCALE
    zs = []
    mx = jnp.full((HEAD_ROWS, page), -jnp.inf, F32)
    for pg in range(npages):
        z = s_all[:, pg * page:(pg + 1) * page] + (bias0 - cum[pg * HEAD_ROWS:(pg + 1) * HEAD_ROWS, :])
        zs.append(z)
        mx = jnp.maximum(mx, z)
    m = jnp.maximum(jnp.max(mx, axis=-1, keepdims=True), s_new)
    ps = [jnp.exp(z - m) for z in zs]
    lsum = ps[0]
    for p in ps[1:]:
        lsum = lsum + p
    p_new = jnp.exp(s_new - m)
    l = jnp.sum(lsum, axis=-1, keepdims=True) + p_new
    pcat = jnp.concatenate([p.astype(BF16) for p in ps], axis=1)
    o = _dot(pcat, vbuf[slot].astype(BF16)) + p_new * vn_ref[0]
    o_ref[0] = o / l


def _fox_decode(pt, q8, kn, vn, lfn, gpre, tpre, kc, vc, lfc, *, npages, page):
    nseq = q8.shape[0]
    hd = FOX_HEAD_DIM
    blk = lambda r, n: pl.BlockSpec((1, r, n), lambda b, pt_ref: (b, 0, 0))
    const = lambda a: pl.BlockSpec(a.shape, lambda b, pt_ref: (0,) * a.ndim)
    grid_spec = pltpu.PrefetchScalarGridSpec(
        num_scalar_prefetch=1,
        grid=(nseq,),
        in_specs=[blk(HEAD_ROWS, hd), blk(1, hd), blk(1, hd), blk(HEAD_ROWS, page),
                  const(gpre), const(tpre),
                  pl.BlockSpec(memory_space=pl.ANY), pl.BlockSpec(memory_space=pl.ANY),
                  pl.BlockSpec(memory_space=pl.ANY)],
        out_specs=blk(HEAD_ROWS, hd),
        scratch_shapes=[
            pltpu.VMEM((DECODE_SLOTS, npages * page, hd), F32),
            pltpu.VMEM((DECODE_SLOTS, npages * page, hd), F32),
            pltpu.VMEM((DECODE_SLOTS, npages * HEAD_ROWS, page), F32),
            pltpu.SemaphoreType.DMA((DECODE_SLOTS, 3)),
        ],
    )
    return pl.pallas_call(
        functools.partial(_fox_decode_body, npages=npages, page=page),
        out_shape=jax.ShapeDtypeStruct((nseq, HEAD_ROWS, hd), F32),
        grid_spec=grid_spec,
        compiler_params=_cparams(("arbitrary",)),
        name="fox_decode",
    )(pt, q8, kn, vn, lfn, gpre, tpre, kc, vc, lfc)


def _mla_decode_body(pt_ref, ql_ref, qr_ref, cn_ref, rn_ref, c_hbm, r_hbm, o_ref,
                     cbuf, rbuf, sems, *, npages, page):
    b = pl.program_id(0)
    nb = pl.num_programs(0)
    slot = b % DECODE_SLOTS

    def fetch(seq, sl):
        def body(pg, carry):
            pid = pt_ref[seq, pg]
            row = pl.multiple_of(pg * page, page)
            pltpu.make_async_copy(c_hbm.at[pid], cbuf.at[sl, pl.ds(row, page), :], sems.at[sl, 0]).start()
            pltpu.make_async_copy(r_hbm.at[pid], rbuf.at[sl, :, pl.ds(row, page)], sems.at[sl, 1]).start()
            return carry
        lax.fori_loop(0, npages, body, 0)

    _ring_prefetch(b, nb, fetch)
    _wait_all(cbuf.at[slot], sems.at[slot, 0])
    _wait_all(rbuf.at[slot], sems.at[slot, 1])

    ql = ql_ref[0]
    qr = qr_ref[0]
    cb = cbuf[slot].astype(BF16)
    rb = rbuf[slot].astype(BF16)
    s = (_dot_nt(ql.astype(BF16), cb) + _dot(qr.astype(BF16), rb)) * MLA_SCALE
    s_new = (jnp.sum(ql * cn_ref[0], axis=-1, keepdims=True)
             + jnp.sum(qr * rn_ref[0], axis=-1, keepdims=True)) * MLA_SCALE
    m = jnp.maximum(jnp.max(s, axis=-1, keepdims=True), s_new)
    p = jnp.exp(s - m)
    p_new = jnp.exp(s_new - m)
    l = jnp.sum(p, axis=-1, keepdims=True) + p_new
    o = _dot(p.astype(BF16), cb) + p_new * cn_ref[0]
    o_ref[0] = o / l


def _mla_decode(pt, ql8, qr8, cn, rn, cc, rc, *, npages, page):
    nseq = ql8.shape[0]
    blk = lambda r, n: pl.BlockSpec((1, r, n), lambda b, pt_ref: (b, 0, 0))
    grid_spec = pltpu.PrefetchScalarGridSpec(
        num_scalar_prefetch=1,
        grid=(nseq,),
        in_specs=[blk(HEAD_ROWS, MLA_KV_RANK), blk(HEAD_ROWS, MLA_ROPE), blk(1, MLA_KV_RANK),
                  blk(1, MLA_ROPE),
                  pl.BlockSpec(memory_space=pl.ANY), pl.BlockSpec(memory_space=pl.ANY)],
        out_specs=blk(HEAD_ROWS, MLA_KV_RANK),
        scratch_shapes=[
            pltpu.VMEM((DECODE_SLOTS, npages * page, MLA_KV_RANK), F32),
            pltpu.VMEM((DECODE_SLOTS, MLA_ROPE, npages * page), F32),
            pltpu.SemaphoreType.DMA((DECODE_SLOTS, 2)),
        ],
    )
    return pl.pallas_call(
        functools.partial(_mla_decode_body, npages=npages, page=page),
        out_shape=jax.ShapeDtypeStruct((nseq, HEAD_ROWS, MLA_KV_RANK), F32),
        grid_spec=grid_spec,
        compiler_params=_cparams(("arbitrary",)),
        name="mla_decode",
    )(pt, ql8, qr8, cn, rn, cc, rc)


def _final_norm_body(x_ref, g_ref, o_ref):
    o_ref[...] = _rms(x_ref[...], g_ref[...])


def _final_norm(x, g, *, tm):
    m, d = x.shape
    return pl.pallas_call(
        _final_norm_body,
        out_shape=jax.ShapeDtypeStruct((m, d), F32),
        grid=(m // tm,),
        in_specs=[pl.BlockSpec((tm, d), lambda i: (i, 0)), pl.BlockSpec((1, d), lambda i: (0, 0))],
        out_specs=pl.BlockSpec((tm, d), lambda i: (i, 0)),
        compiler_params=_cparams(("parallel",)),
        name="final_norm",
    )(x, g)


def _rot_cols(w):
    shp = w.shape
    w4 = w.reshape(shp[:-1] + (shp[-1] // MLA_ROPE, 2, MLA_ROPE // 2))
    return w4[..., ::-1, :].reshape(shp)


def _prep_stacked(p):
    s = {}
    for k in ("1", "2"):
        s["g" + k] = p[f"w_ffn{k}_gate"].astype(BF16)
        s["u" + k] = p[f"w_ffn{k}_up"].astype(BF16)
        s["d" + k] = p[f"w_ffn{k}_down"].astype(BF16)
    wt = jnp.transpose(p["w_in"], (0, 2, 1))
    offs = np.cumsum((0, 512, 128, 128, 4, 512, 512, 512, 512, 256, 64, 512))
    fq, fk, fv, ff, cv, cb, cc, mq, mkv, mkr, su = [wt[:, offs[i]:offs[i + 1], :] for i in range(11)]
    half = MLA_ROPE // 2
    mkr_rot = jnp.concatenate([mkr[:, half:, :], mkr[:, :half, :]], axis=1)
    zeros = jnp.zeros((wt.shape[0], LANE - FOX_HEADS, wt.shape[2]), F32)
    s["w_in_t"] = jnp.concatenate(
        [fq, fk, fv, cv, cb, cc, mq, mkv, su, mkr, mkr_rot, ff, zeros], axis=1).astype(BF16)
    s["w_out"] = p["w_out"].astype(BF16)
    return s


def _prep_layer(l, p):
    w = {}
    for k in ("1", "2"):
        w["n" + k] = p[f"norm_ffn{k}"][l][None, :]
    w["norm_mix"] = p["norm_mix"][l][None, :]
    w["bias_f"] = jnp.pad(p["fox_b_f"][l], (0, LANE - FOX_HEADS))[None, :]
    wuq = p["mla_w_uq"][l].reshape(MLA_Q_RANK, MLA_HEADS, MLA_NOPE + MLA_ROPE)
    w["wq_nope"] = wuq[:, :, :MLA_NOPE].reshape(MLA_Q_RANK, MLA_HEADS * MLA_NOPE).astype(BF16)
    wr = wuq[:, :, MLA_NOPE:].reshape(MLA_Q_RANK, MLA_HEADS * MLA_ROPE)
    w["wq_rope"] = wr.astype(BF16)
    w["wq_rope_rot"] = _rot_cols(wr).astype(BF16)
    w["w_uk"] = jnp.transpose(p["mla_w_uk"][l], (1, 2, 0)).astype(BF16)
    w["w_uv"] = jnp.transpose(p["mla_w_uv"][l], (1, 0, 2)).astype(BF16)
    w["g_q"] = p["mla_norm_q"][l][None, :]
    w["g_kv"] = p["mla_norm_kv"][l][None, :]
    lr, li = p["s5_lambda_re"][l], p["s5_lambda_im"][l]
    step = jnp.exp(p["s5_log_step"][l])[:, None]
    mag = jnp.exp(lr * step)
    lbr, lbi = mag * jnp.cos(li * step), mag * jnp.sin(li * step)
    den = lr * lr + li * li
    cr = ((lbr - 1.0) * lr + lbi * li) / den
    ci = (lbi * lr - (lbr - 1.0) * li) / den
    bbr = cr[..., None] * p["s5_b_re"][l] - ci[..., None] * p["s5_b_im"][l]
    bbi = cr[..., None] * p["s5_b_im"][l] + ci[..., None] * p["s5_b_re"][l]
    eye = jnp.eye(S5_GROUPS, dtype=F32)
    bmat = lambda t: jnp.einsum("gnc,gh->gchn", t, eye).reshape(GROUP_WIDTH, S5_LANES)
    cmat = lambda t: jnp.einsum("gcn,gh->gnhc", t, eye).reshape(S5_LANES, GROUP_WIDTH)
    bre, bim = bmat(bbr), bmat(bbi)
    w["bre"], w["bim"] = bre.astype(BF16), bim.astype(BF16)
    w["bre_lo"] = (bre - w["bre"].astype(F32)).astype(BF16)
    w["bim_lo"] = (bim - w["bim"].astype(F32)).astype(BF16)
    w["cre"] = cmat(p["s5_c_re"][l]).astype(BF16)
    w["cim"] = cmat(-p["s5_c_im"][l]).astype(BF16)
    pr, pi = [lbr.reshape(1, S5_LANES)], [lbi.reshape(1, S5_LANES)]
    for _ in range(15):
        r, i = pr[-1], pi[-1]
        pr.append(r * r - i * i)
        pi.append(2.0 * r * i)
    w["pow_re"], w["pow_im"] = jnp.concatenate(pr, axis=0), jnp.concatenate(pi, axis=0)
    w["s5_d"] = p["s5_d"][l][None, :]
    w["w_glu"] = p["s5_w_glu"][l].astype(BF16)
    w["gains"] = p["norm_group_out"][l].reshape(4, GROUP_WIDTH)
    w["conv_w"] = jnp.pad(p["conv_w"][l], ((0, SUBLANE - CONV_WIDTH), (0, 0)))
    return w


def _rope_tables(pos):
    half = MLA_ROPE // 2
    inv_freq = ROPE_THETA ** (-jnp.arange(half, dtype=F32) / half)
    ang = pos.astype(F32)[:, None] * inv_freq
    cos, sin = jnp.cos(ang), jnp.sin(ang)
    cos64 = jnp.concatenate([cos, cos], axis=1)
    sin64 = jnp.concatenate([-sin, sin], axis=1)
    return jnp.tile(cos64, (1, MLA_HEADS)), jnp.tile(sin64, (1, MLA_HEADS))


def _decode_consts(npages, page):
    r = np.arange(npages * HEAD_ROWS)
    c = np.arange(page)
    gpre = (c[:, None] <= c[None, :]).astype(np.float32)
    tpre = (((r % HEAD_ROWS)[:, None] == (r % HEAD_ROWS)[None, :])
            & ((r // HEAD_ROWS)[None, :] < (r // HEAD_ROWS)[:, None])).astype(np.float32)
    return jnp.asarray(gpre, BF16), jnp.asarray(tpre, BF16)


def _pick(m, pref):
    for t in pref:
        if m % t == 0:
            return t
    return m


def kernel(x_prompt, x_sample, cache_fox_k, cache_fox_v, cache_fox_logf, cache_mla_ckv, cache_mla_krope, state_conv, state_s5_re, state_s5_im, page_table, norm_ffn1, w_ffn1_gate, w_ffn1_up, w_ffn1_down, norm_mix, w_in, fox_b_f, conv_w, mla_norm_q, mla_w_uq, mla_norm_kv, mla_w_uk, mla_w_uv, s5_lambda_re, s5_lambda_im, s5_log_step, s5_b_re, s5_b_im, s5_c_re, s5_c_im, s5_d, s5_w_glu, norm_group_out, w_out, norm_ffn2, w_ffn2_gate, w_ffn2_up, w_ffn2_down, norm_final):
    p = dict(norm_ffn1=norm_ffn1, w_ffn1_gate=w_ffn1_gate, w_ffn1_up=w_ffn1_up,
             w_ffn1_down=w_ffn1_down, norm_mix=norm_mix, w_in=w_in, fox_b_f=fox_b_f, conv_w=conv_w,
             mla_norm_q=mla_norm_q, mla_w_uq=mla_w_uq, mla_norm_kv=mla_norm_kv, mla_w_uk=mla_w_uk,
             mla_w_uv=mla_w_uv, s5_lambda_re=s5_lambda_re, s5_lambda_im=s5_lambda_im,
             s5_log_step=s5_log_step, s5_b_re=s5_b_re, s5_b_im=s5_b_im, s5_c_re=s5_c_re,
             s5_c_im=s5_c_im, s5_d=s5_d, s5_w_glu=s5_w_glu, norm_group_out=norm_group_out,
             w_out=w_out, norm_ffn2=norm_ffn2, w_ffn2_gate=w_ffn2_gate, w_ffn2_up=w_ffn2_up,
             w_ffn2_down=w_ffn2_down)
    depth = w_in.shape[0]
    bp, sp, dm = x_prompt.shape
    bd, sd, _ = x_sample.shape
    assert sd == 1, "sample group is a single-token decode step"
    n_pool, page = cache_fox_k.shape[1], cache_fox_k.shape[2]
    npages = page_table.shape[1]
    n_past = npages * page
    mp = bp * sp
    tf = 512

    tm_p = _pick(mp, (512, 256, 128))
    tm_d = bd
    tq = _pick(sp, (512, 256, 128))
    lt = _pick(sp, (128,))

    cos_p, sin_p = _rope_tables(jnp.arange(sp))
    cos_d, sin_d = _rope_tables(jnp.full((bd,), n_past))
    gpre, tpre = _decode_consts(npages, page)

    kc = cache_fox_k.reshape(depth * n_pool, page, FOX_HEAD_DIM)
    vc = cache_fox_v.reshape(depth * n_pool, page, FOX_HEAD_DIM)
    lfc = cache_fox_logf.transpose(0, 1, 3, 2).reshape(depth * n_pool, FOX_HEADS, page)
    ckc = cache_mla_ckv.reshape(depth * n_pool, page, MLA_KV_RANK)
    krc = cache_mla_krope.transpose(0, 1, 3, 2).reshape(depth * n_pool, MLA_ROPE, page)

    xp = x_prompt.reshape(mp, dm)
    xd = x_sample.reshape(bd, dm)
    acc = {k: [] for k in ("fox_k_p", "fox_v_p", "fox_logf_p", "mla_ckv_p", "mla_krope_p", "conv_p",
                           "s5_re_p", "s5_im_p", "fox_k_s", "fox_v_s", "fox_logf_s", "mla_ckv_s",
                           "mla_krope_s", "conv_s", "s5_re_s", "s5_im_s")}
    pad_heads = lambda t: jnp.pad(t, ((0, 0), (0, HEAD_ROWS - t.shape[1]), (0, 0)))

    ws = _prep_stacked(p)
    for l in range(depth):
        w = _prep_layer(l, p)
        xp = _ffn(xp, w["n1"], ws["g1"], ws["u1"], ws["d1"], l, tm=tm_p, tf=tf)
        fq, fk, fv, cv, cb, cc, mq, mkv, su, kr, lf = _inproj(xp, w["norm_mix"], ws["w_in_t"], w["bias_f"], l,
                                                              tm=_pick(mp, (256, 128)))
        logf = lf[:, :FOX_HEADS]
        cum = _cumsum_lanes(logf.reshape(bp, sp, FOX_HEADS).transpose(0, 2, 1).reshape(bp * FOX_HEADS, sp))
        ck = cum.reshape(bp, FOX_HEADS, sp)
        cq = jnp.broadcast_to(ck.transpose(0, 2, 1)[..., None], (bp, sp, FOX_HEADS, LANE))
        cq = cq.reshape(mp, FOX_HEADS * LANE)
        a_p = _fox_prompt(fq, fk, fv, cq, ck, nb=bp, seq=sp, tq=tq, tk=tq)
        ql, qr, ckv, ckvb, krope, kropeb = _mla_proj(
            mq, mkv, kr, cos_p, sin_p, w["g_q"], w["g_kv"], w["wq_nope"], w["wq_rope"],
            w["wq_rope_rot"], w["w_uk"], tm=_pick(sp, (256, 128)))
        c_p = _mla_prompt(ql, qr, ckvb, kropeb, w["w_uv"], nb=bp, seq=sp, tq=tq, tk=tq)
        d_p, sre_p, sim_p = _s5_prompt(su, w["bre"], w["bim"], w["cre"], w["cim"], w["pow_re"],
                                       w["pow_im"], w["s5_d"], w["w_glu"], nb=bp, seq=sp, lt=lt)
        xp, cs_p = _postmix_prompt(xp, a_p, cv, cb, cc, c_p, d_p, w["gains"], w["conv_w"], ws["w_out"], l,
                                   seq=sp, tm=_pick(sp, (256, 128)))
        xp = _ffn(xp, w["n2"], ws["g2"], ws["u2"], ws["d2"], l, tm=tm_p, tf=tf)
        acc["fox_k_p"].append(fk.reshape(bp, sp, 1, FOX_HEAD_DIM))
        acc["fox_v_p"].append(fv.reshape(bp, sp, 1, FOX_HEAD_DIM))
        acc["fox_logf_p"].append(logf.reshape(bp, sp, FOX_HEADS))
        acc["mla_ckv_p"].append(ckv.reshape(bp, sp, MLA_KV_RANK))
        acc["mla_krope_p"].append(krope.reshape(bp, sp, MLA_ROPE))
        acc["conv_p"].append(cs_p[:, SUBLANE - (CONV_WIDTH - 1):, :])
        acc["s5_re_p"].append(sre_p[:, SUBLANE - 1, :].reshape(bp, S5_GROUPS, S5_STATE))
        acc["s5_im_p"].append(sim_p[:, SUBLANE - 1, :].reshape(bp, S5_GROUPS, S5_STATE))

        xd = _ffn(xd, w["n1"], ws["g1"], ws["u1"], ws["d1"], l, tm=tm_d, tf=tf)
        fq, fk, fv, cv, cb, cc, mq, mkv, su, kr, lf = _inproj(xd, w["norm_mix"], ws["w_in_t"], w["bias_f"], l,
                                                              tm=tm_d)
        logf = lf[:, :FOX_HEADS]
        pt = page_table + l * n_pool
        q8 = pad_heads(fq.reshape(bd, FOX_HEADS, FOX_HEAD_DIM))
        lfn = jnp.broadcast_to(pad_heads(logf[:, :, None]), (bd, HEAD_ROWS, page))
        a8 = _fox_decode(pt, q8, fk[:, None, :], fv[:, None, :], lfn, gpre, tpre, kc, vc, lfc,
                         npages=npages, page=page)
        a_d = a8[:, :FOX_HEADS, :].reshape(bd, GROUP_WIDTH)
        ql, qr, ckv, ckvb, krope, kropeb = _mla_proj(
            mq, mkv, kr, cos_d, sin_d, w["g_q"], w["g_kv"], w["wq_nope"], w["wq_rope"],
            w["wq_rope_rot"], w["w_uk"], tm=tm_d)
        ql8 = pad_heads(ql.astype(F32).reshape(bd, MLA_HEADS, MLA_KV_RANK))
        qr8 = pad_heads(qr.astype(F32).reshape(bd, MLA_HEADS, MLA_ROPE))
        ol8 = _mla_decode(pt, ql8, qr8, ckv[:, None, :], krope[:, None, :], ckc, krc,
                          npages=npages, page=page)
        olat = ol8[:, :MLA_HEADS, :].reshape(bd, MLA_HEADS * MLA_KV_RANK)
        sc = state_conv[l]
        xd, xc_d, sre_d, sim_d = _postmix_sample(
            xd, a_d, cv, cb, cc, sc[:, 0, :], sc[:, 1, :], olat, su,
            state_s5_re[l].reshape(bd, S5_LANES), state_s5_im[l].reshape(bd, S5_LANES),
            w["gains"], w["conv_w"], ws["w_out"], w["w_uv"], w["bre"], w["bre_lo"], w["bim"], w["bim_lo"],
            w["cre"], w["cim"], w["pow_re"][0:1], w["pow_im"][0:1], w["s5_d"], w["w_glu"], l)
        xd = _ffn(xd, w["n2"], ws["g2"], ws["u2"], ws["d2"], l, tm=tm_d, tf=tf)
        acc["fox_k_s"].append(fk.reshape(bd, 1, 1, FOX_HEAD_DIM))
        acc["fox_v_s"].append(fv.reshape(bd, 1, 1, FOX_HEAD_DIM))
        acc["fox_logf_s"].append(logf.reshape(bd, 1, FOX_HEADS))
        acc["mla_ckv_s"].append(ckv.reshape(bd, 1, MLA_KV_RANK))
        acc["mla_krope_s"].append(krope.reshape(bd, 1, MLA_ROPE))
        acc["conv_s"].append(jnp.stack([sc[:, 1, :], xc_d], axis=1))
        acc["s5_re_s"].append(sre_d.reshape(bd, S5_GROUPS, S5_STATE))
        acc["s5_im_s"].append(sim_d.reshape(bd, S5_GROUPS, S5_STATE))

    gfin = norm_final[None, :]
    y_prompt = _final_norm(xp, gfin, tm=tm_p).reshape(bp, sp, dm)
    y_sample = _final_norm(xd, gfin, tm=tm_d).reshape(bd, sd, dm)
    st = {k: jnp.stack(v) for k, v in acc.items()}
    return (y_prompt, y_sample,
            st["fox_k_p"], st["fox_v_p"], st["fox_logf_p"], st["mla_ckv_p"], st["mla_krope_p"],
            st["conv_p"], st["s5_re_p"], st["s5_im_p"],
            st["fox_k_s"], st["fox_v_s"], st["fox_logf_s"], st["mla_ckv_s"], st["mla_krope_s"],
            st["conv_s"], st["s5_re_s"], st["s5_im_s"])
```

```python
import functools
import math

import jax
import jax.numpy as jnp
import numpy as np
from jax import lax
from jax.experimental import pallas as pl
from jax.experimental.pallas import tpu as pltpu

F32 = jnp.float32
BF16 = jnp.bfloat16

EPS = 1e-6
FOX_HEADS = 4
FOX_HEAD_DIM = 128
FOX_SCALE = FOX_HEAD_DIM ** -0.5
GROUP_WIDTH = 512
MLA_HEADS = 4
MLA_NOPE = 128
MLA_ROPE = 64
MLA_V = 128
MLA_Q_RANK = 512
MLA_KV_RANK = 256
MLA_SCALE = (MLA_NOPE + MLA_ROPE) ** -0.5
ROPE_THETA = 10000.0
S5_GROUPS = 32
S5_GROUP_SIZE = 16
S5_STATE = 64
S5_LANES = S5_GROUPS * S5_STATE
CONV_WIDTH = 3
LANE = 128
SUBLANE = 8
HEAD_ROWS = 8
VMEM_LIMIT = 56 * 1024 * 1024

_SEG = dict(fq=(0, 512), fkv=(512, 256),
            cv=(0, 512), cb=(512, 512), cc=(1024, 512), mq=(1536, 512), mkv=(2048, 256),
            kr=(2304, 64), su=(2368, 512))
FF_LANE0 = MLA_ROPE


def _cparams(sem, vmem=VMEM_LIMIT):
    return pltpu.CompilerParams(dimension_semantics=sem, vmem_limit_bytes=vmem)


def _rms(x, g):
    ms = jnp.mean(x * x, axis=-1, keepdims=True)
    return x * lax.rsqrt(ms + EPS) * g


def _dot(a, b):
    return jnp.dot(a, b, preferred_element_type=F32)


def _dot_nt(a, b):
    return lax.dot_general(a, b, (((1,), (1,)), ((), ())), preferred_element_type=F32)


def _const_spec(shape):
    nd = len(shape)
    return pl.BlockSpec(shape, lambda *_: (0,) * nd, pipeline_mode=pl.Buffered(1))


def _ffn_body(x_ref, g_ref, wg_ref, wu_ref, wd_ref, o_ref, h_ref, *, tf, tail):
    f = pl.program_id(1)
    nf = pl.num_programs(1)

    @pl.when(f == 0)
    def _():
        x = x_ref[...]
        h_ref[...] = _rms(x, g_ref[...]).astype(BF16)
        o_ref[...] = x

    def accumulate(n):
        h = h_ref[...]
        a = _dot(h, wg_ref[:, :n])
        b = _dot(h, wu_ref[:, :n])
        act = (a * jax.nn.sigmoid(a) * b).astype(BF16)
        o_ref[...] += 0.5 * _dot(act, wd_ref[:n, :])

    if tail == tf:
        accumulate(tf)
    else:
        pl.when(f < nf - 1)(lambda: accumulate(tf))
        pl.when(f == nf - 1)(lambda: accumulate(tail))


def _ffn(x, g, wg, wu, wd, l, *, tm, tf):
    m, d = x.shape
    d_ff = wg.shape[2]
    nf = pl.cdiv(d_ff, tf)
    tail = d_ff - (nf - 1) * tf
    assert tail % LANE == 0
    return pl.pallas_call(
        functools.partial(_ffn_body, tf=tf, tail=tail),
        out_shape=jax.ShapeDtypeStruct((m, d), F32),
        grid=(m // tm, nf),
        in_specs=[
            pl.BlockSpec((tm, d), lambda i, f: (i, 0)),
            pl.BlockSpec((1, d), lambda i, f: (0, 0)),
            pl.BlockSpec((None, d, tf), lambda i, f: (l, 0, f)),
            pl.BlockSpec((None, d, tf), lambda i, f: (l, 0, f)),
            pl.BlockSpec((None, tf, d), lambda i, f: (l, f, 0)),
        ],
        out_specs=pl.BlockSpec((tm, d), lambda i, f: (i, 0)),
        scratch_shapes=[pltpu.VMEM((tm, d), BF16)],
        compiler_params=_cparams(("parallel", "arbitrary")),
        name="ffn",
    )(x, g, wg, wu, wd)


def _inproj_body(x_ref, g_ref, wa_ref, wb_ref, wc_ref, bias_ref, fq_ref, fk_ref, fv_ref, cv_ref, cb_ref,
                 cc_ref, mq_ref, mkv_ref, su_ref, kr_ref, krr_ref, lf_ref):
    h = _rms(x_ref[...], g_ref[...]).astype(BF16)

    def seg(ref, name):
        a, n = _SEG[name]
        return _dot_nt(h, ref[a:a + n, :])

    fq_ref[...] = seg(wa_ref, "fq")
    fkv = seg(wa_ref, "fkv")
    fk_ref[...] = fkv[:, :FOX_HEAD_DIM]
    fv_ref[...] = fkv[:, FOX_HEAD_DIM:]
    cv_ref[...] = seg(wb_ref, "cv")
    cb_ref[...] = seg(wb_ref, "cb")
    cc_ref[...] = seg(wb_ref, "cc")
    mq_ref[...] = seg(wb_ref, "mq")
    mkv_ref[...] = seg(wb_ref, "mkv")
    kr_ref[...] = seg(wb_ref, "kr")
    su_ref[...] = seg(wb_ref, "su")
    zc = _dot_nt(h, wc_ref[...])
    krr_ref[...] = zc[:, :MLA_ROPE]
    zf = zc + bias_ref[...]
    lf_ref[...] = -(jnp.maximum(-zf, 0.0) + jnp.log1p(jnp.exp(-jnp.abs(zf))))


def _inproj(x, g, wa, wb, wc, bias, l, *, tm):
    m, d = x.shape
    widths = [512, 128, 128, 512, 512, 512, 512, 256, 512, MLA_ROPE, MLA_ROPE, LANE]
    wspec = lambda w: pl.BlockSpec((None,) + w.shape[1:], lambda i: (l, 0, 0),
                                   pipeline_mode=pl.Buffered(1))
    return pl.pallas_call(
        _inproj_body,
        out_shape=[jax.ShapeDtypeStruct((m, n), F32) for n in widths],
        grid=(m // tm,),
        in_specs=[
            pl.BlockSpec((tm, d), lambda i: (i, 0)),
            pl.BlockSpec((1, d), lambda i: (0, 0)),
            wspec(wa), wspec(wb), wspec(wc),
            pl.BlockSpec((1, LANE), lambda i: (0, 0)),
        ],
        out_specs=[pl.BlockSpec((tm, n), lambda i: (i, 0)) for n in widths],
        compiler_params=_cparams(("parallel",)),
        name="inproj",
    )(x, g, wa, wb, wc, bias)


def _cumsum_body(x_ref, o_ref):
    x = x_ref[...]
    n = x.shape[1]
    lane = lax.broadcasted_iota(jnp.int32, x.shape, 1)
    d = 1
    while d < n:
        x = x + jnp.where(lane >= d, pltpu.roll(x, d, axis=1), 0.0)
        d *= 2
    o_ref[...] = x


def _cumsum_lanes(x):
    return pl.pallas_call(
        _cumsum_body, out_shape=jax.ShapeDtypeStruct(x.shape, F32), name="logf_cumsum")(x)


def _lanes(x, reps):
    return x if reps == 1 else jnp.concatenate([x] * reps, axis=1)


def _flash_init(m_ref, l_ref, acc_ref):
    m_ref[...] = jnp.full(m_ref.shape, -jnp.inf, F32)
    l_ref[...] = jnp.zeros(l_ref.shape, F32)
    acc_ref[...] = jnp.zeros(acc_ref.shape, F32)


def _flash_update(s, vb, m_ref, l_ref, acc_ref, h, cols):
    tk = s.shape[1]
    m_prev = m_ref[h]
    m_new = jnp.maximum(m_prev, jnp.max(s, axis=-1, keepdims=True))
    alpha = jnp.exp(m_prev - m_new)
    p = jnp.exp(s - _lanes(m_new, tk // LANE))
    psum = p[:, :LANE]
    for c in range(1, tk // LANE):
        psum = psum + p[:, c * LANE:(c + 1) * LANE]
    l_ref[h] = alpha * l_ref[h] + psum
    width = cols.stop - cols.start
    acc_ref[:, cols] = _lanes(alpha, width // LANE) * acc_ref[:, cols] + _dot(p.astype(BF16), vb)
    m_ref[h] = m_new


def _causal_mask(tq, tk):
    return (lax.broadcasted_iota(jnp.int32, (tq, tk), 1) <= lax.broadcasted_iota(jnp.int32, (tq, tk), 0))


def _fox_prompt_body(q_ref, k_ref, v_ref, cq_ref, ck_ref, o_ref, m_ref, l_ref, acc_ref, *, tq, tk):
    i = pl.program_id(1)
    j = pl.program_id(2)

    @pl.when(j == 0)
    def _():
        _flash_init(m_ref, l_ref, acc_ref)

    def step(diagonal):
        kb = k_ref[...].astype(BF16)
        vb = v_ref[...].astype(BF16)
        for h in range(FOX_HEADS):
            hs = slice(h * FOX_HEAD_DIM, (h + 1) * FOX_HEAD_DIM)
            s = _dot_nt(q_ref[:, hs].astype(BF16), kb) * FOX_SCALE
            s = s + _lanes(cq_ref[:, hs], tk // LANE) - ck_ref[0, h:h + 1, :]
            if diagonal:
                s = jnp.where(_causal_mask(tq, tk), s, -jnp.inf)
            _flash_update(s, vb, m_ref, l_ref, acc_ref, h, hs)

    pl.when(j < i)(lambda: step(False))
    pl.when(j == i)(lambda: step(True))

    @pl.when(j == pl.num_programs(2) - 1)
    def _():
        for h in range(FOX_HEADS):
            hs = slice(h * FOX_HEAD_DIM, (h + 1) * FOX_HEAD_DIM)
            o_ref[:, hs] = acc_ref[:, hs] / jnp.sum(l_ref[h], axis=-1, keepdims=True)


def _fox_prompt(q, k, v, cq, ck, *, nb, seq, tq, tk):
    assert tq == tk
    nq, nk = seq // tq, seq // tk
    kv_idx = lambda b, i, j: (b * nk + jnp.minimum(j, i), 0)
    return pl.pallas_call(
        functools.partial(_fox_prompt_body, tq=tq, tk=tk),
        out_shape=jax.ShapeDtypeStruct(q.shape, F32),
        grid=(nb, nq, nk),
        in_specs=[
            pl.BlockSpec((tq, q.shape[1]), lambda b, i, j: (b * nq + i, 0)),
            pl.BlockSpec((tk, FOX_HEAD_DIM), kv_idx),
            pl.BlockSpec((tk, FOX_HEAD_DIM), kv_idx),
            pl.BlockSpec((tq, cq.shape[1]), lambda b, i, j: (b * nq + i, 0)),
            pl.BlockSpec((1, FOX_HEADS, tk), lambda b, i, j: (b, 0, jnp.minimum(j, i))),
        ],
        out_specs=pl.BlockSpec((tq, q.shape[1]), lambda b, i, j: (b * nq + i, 0)),
        scratch_shapes=[
            pltpu.VMEM((FOX_HEADS, tq, LANE), F32),
            pltpu.VMEM((FOX_HEADS, tq, LANE), F32),
            pltpu.VMEM((tq, q.shape[1]), F32),
        ],
        compiler_params=_cparams(("parallel", "parallel", "arbitrary")),
        name="fox_prompt",
    )(q, k, v, cq, ck)


def _mla_proj_body(mq_ref, mkv_ref, kr_ref, krr_ref, cos_ref, sin_ref, gq_ref, gkv_ref, wn_ref, wr_ref,
                   wrr_ref, wuk_ref, ql_ref, qr_ref, ckv_ref, ckvb_ref, krope_ref, kropeb_ref):
    cq = _rms(mq_ref[...], gq_ref[...]).astype(BF16)
    cos = cos_ref[...]
    sin = sin_ref[...]
    qr_ref[...] = (_dot(cq, wr_ref[...]) * cos + _dot(cq, wrr_ref[...]) * sin).astype(BF16)
    qn = _dot(cq, wn_ref[...])
    for h in range(MLA_HEADS):
        qh = qn[:, h * MLA_NOPE:(h + 1) * MLA_NOPE].astype(BF16)
        ql_ref[:, h * MLA_KV_RANK:(h + 1) * MLA_KV_RANK] = _dot(qh, wuk_ref[h]).astype(BF16)
    ckv = _rms(mkv_ref[...], gkv_ref[...])
    ckv_ref[...] = ckv
    ckvb_ref[...] = ckv.astype(BF16)
    krope = kr_ref[...] * cos[:, :MLA_ROPE] + krr_ref[...] * sin[:, :MLA_ROPE]
    krope_ref[...] = krope
    kropeb_ref[...] = krope.astype(BF16)


def _mla_proj(mq, mkv, kr, krr, cos, sin, gq, gkv, wn, wr, wrr, wuk, *, tm):
    m = mq.shape[0]
    ntab = cos.shape[0] // tm
    row = lambda n: pl.BlockSpec((tm, n), lambda i: (i, 0))
    tab = pl.BlockSpec((tm, cos.shape[1]), lambda i: (i % ntab, 0))
    return pl.pallas_call(
        _mla_proj_body,
        out_shape=[
            jax.ShapeDtypeStruct((m, MLA_HEADS * MLA_KV_RANK), BF16),
            jax.ShapeDtypeStruct((m, MLA_HEADS * MLA_ROPE), BF16),
            jax.ShapeDtypeStruct((m, MLA_KV_RANK), F32),
            jax.ShapeDtypeStruct((m, MLA_KV_RANK), BF16),
            jax.ShapeDtypeStruct((m, MLA_ROPE), F32),
            jax.ShapeDtypeStruct((m, MLA_ROPE), BF16),
        ],
        grid=(m // tm,),
        in_specs=[row(MLA_Q_RANK), row(MLA_KV_RANK), row(MLA_ROPE), row(MLA_ROPE), tab, tab,
                  pl.BlockSpec((1, MLA_Q_RANK), lambda i: (0, 0)),
                  pl.BlockSpec((1, MLA_KV_RANK), lambda i: (0, 0)),
                  _const_spec(wn.shape), _const_spec(wr.shape), _const_spec(wrr.shape),
                  _const_spec(wuk.shape)],
        out_specs=[row(MLA_HEADS * MLA_KV_RANK), row(MLA_HEADS * MLA_ROPE), row(MLA_KV_RANK),
                   row(MLA_KV_RANK), row(MLA_ROPE), row(MLA_ROPE)],
        compiler_params=_cparams(("parallel",)),
        name="mla_proj",
    )(mq, mkv, kr, krr, cos, sin, gq, gkv, wn, wr, wrr, wuk)


def _mla_prompt_body(ql_ref, qr_ref, ckv_ref, kr_ref, wuv_ref, o_ref, m_ref, l_ref, acc_ref,
                     *, tq, tk):
    i = pl.program_id(1)
    j = pl.program_id(2)

    @pl.when(j == 0)
    def _():
        _flash_init(m_ref, l_ref, acc_ref)

    def step(diagonal):
        cb = ckv_ref[...]
        kb = kr_ref[...]
        for h in range(MLA_HEADS):
            ls = slice(h * MLA_KV_RANK, (h + 1) * MLA_KV_RANK)
            rs = slice(h * MLA_ROPE, (h + 1) * MLA_ROPE)
            s = (_dot_nt(ql_ref[:, ls], cb) + _dot_nt(qr_ref[:, rs], kb)) * MLA_SCALE
            if diagonal:
                s = jnp.where(_causal_mask(tq, tk), s, -jnp.inf)
            _flash_update(s, cb, m_ref, l_ref, acc_ref, h, ls)

    pl.when(j < i)(lambda: step(False))
    pl.when(j == i)(lambda: step(True))

    @pl.when(j == pl.num_programs(2) - 1)
    def _():
        for h in range(MLA_HEADS):
            ls = slice(h * MLA_KV_RANK, (h + 1) * MLA_KV_RANK)
            o_lat = (acc_ref[:, ls] / jnp.sum(l_ref[h], axis=-1, keepdims=True)).astype(BF16)
            o_ref[:, h * MLA_V:(h + 1) * MLA_V] = _dot(o_lat, wuv_ref[h])


def _mla_prompt(ql, qr, ckvb, krb, wuv, *, nb, seq, tq, tk):
    assert tq == tk
    nq, nk = seq // tq, seq // tk
    kv_idx = lambda b, i, j: (b * nk + jnp.minimum(j, i), 0)
    m = ql.shape[0]
    return pl.pallas_call(
        functools.partial(_mla_prompt_body, tq=tq, tk=tk),
        out_shape=jax.ShapeDtypeStruct((m, MLA_HEADS * MLA_V), F32),
        grid=(nb, nq, nk),
        in_specs=[
            pl.BlockSpec((tq, ql.shape[1]), lambda b, i, j: (b * nq + i, 0)),
            pl.BlockSpec((tq, qr.shape[1]), lambda b, i, j: (b * nq + i, 0)),
            pl.BlockSpec((tk, MLA_KV_RANK), kv_idx),
            pl.BlockSpec((tk, MLA_ROPE), kv_idx),
            _const_spec(wuv.shape),
        ],
        out_specs=pl.BlockSpec((tq, MLA_HEADS * MLA_V), lambda b, i, j: (b * nq + i, 0)),
        scratch_shapes=[
            pltpu.VMEM((MLA_HEADS, tq, LANE), F32),
            pltpu.VMEM((MLA_HEADS, tq, LANE), F32),
            pltpu.VMEM((tq, MLA_HEADS * MLA_KV_RANK), F32),
        ],
        compiler_params=_cparams(("parallel", "parallel", "arbitrary")),
        name="mla_prompt",
    )(ql, qr, ckvb, krb, wuv)


def _gelu_tanh(y):
    return 0.5 * y * (1.0 + jnp.tanh(math.sqrt(2.0 / math.pi) * (y + 0.044715 * (y * y * y))))


def _cmul_add(xr, xi, ar, ai, sr, si):
    return xr + (ar * sr - ai * si), xi + (ar * si + ai * sr)


def _shifted_scan(xr, xi, rowsel, pre_ref, pim_ref, first_pow, nsteps, lanes):
    for k in range(nsteps):
        d = 1 << k
        keep = rowsel >= d
        sr = jnp.where(keep, pltpu.roll(xr, d, axis=0), 0.0)
        si = jnp.where(keep, pltpu.roll(xi, d, axis=0), 0.0)
        xr, xi = _cmul_add(xr, xi, pre_ref[pl.ds(first_pow + k, 1), lanes],
                           pim_ref[pl.ds(first_pow + k, 1), lanes], sr, si)
    return xr, xi


def _s5_prompt_body(u_ref, bre_ref, bim_ref, cre_ref, cim_ref, pre_ref, pim_ref, p8re_ref, p8im_ref,
                    d_ref, wg_ref, o_ref, sre_ref, sim_ref, xre_ref, xim_ref, car_re, car_im,
                    tre_ref, tim_ref, *, lt):
    t = pl.program_id(1)
    u = u_ref[...]
    ub = u.astype(BF16)
    xre_ref[...] = _dot(ub, bre_ref[...])
    xim_ref[...] = _dot(ub, bim_ref[...])

    @pl.when(t == 0)
    def _():
        car_re[...] = jnp.zeros(car_re.shape, F32)
        car_im[...] = jnp.zeros(car_im.shape, F32)

    ng = lt // SUBLANE
    sub_steps = int(math.log2(SUBLANE))
    grp_steps = int(math.log2(ng))
    sub = lax.broadcasted_iota(jnp.int32, (lt, LANE), 0) & (SUBLANE - 1)
    grow = lax.broadcasted_iota(jnp.int32, (ng, LANE), 0)
    first = grow == 0

    def lane_block(c, carry):
        lanes = pl.ds(pl.multiple_of(c * LANE, LANE), LANE)
        xr, xi = _shifted_scan(xre_ref[:, lanes], xim_ref[:, lanes], sub, pre_ref, pim_ref, 0,
                               sub_steps, lanes)
        tre_ref[...] = xr
        tim_ref[...] = xi
        gr = tre_ref[pl.ds(SUBLANE - 1, ng, stride=SUBLANE), :]
        gi = tim_ref[pl.ds(SUBLANE - 1, ng, stride=SUBLANE), :]
        cin_r = car_re[0:1, lanes]
        cin_i = car_im[0:1, lanes]
        a8r = pre_ref[pl.ds(sub_steps, 1), lanes]
        a8i = pim_ref[pl.ds(sub_steps, 1), lanes]
        gr = gr + jnp.where(first, a8r * cin_r - a8i * cin_i, 0.0)
        gi = gi + jnp.where(first, a8r * cin_i + a8i * cin_r, 0.0)
        gr, gi = _shifted_scan(gr, gi, grow, pre_ref, pim_ref, sub_steps, grp_steps, lanes)
        er = jnp.where(first, cin_r, pltpu.roll(gr, 1, axis=0))
        ei = jnp.where(first, cin_i, pltpu.roll(gi, 1, axis=0))
        p8r = p8re_ref[:, lanes]
        p8i = p8im_ref[:, lanes]
        outs_r, outs_i = [], []
        for v in range(ng):
            rows = slice(v * SUBLANE, (v + 1) * SUBLANE)
            nr, ni = _cmul_add(xr[rows], xi[rows], p8r, p8i,
                               jnp.broadcast_to(er[v:v + 1], (SUBLANE, LANE)),
                               jnp.broadcast_to(ei[v:v + 1], (SUBLANE, LANE)))
            outs_r.append(nr)
            outs_i.append(ni)
        xre_ref[:, lanes] = jnp.concatenate(outs_r, axis=0)
        xim_ref[:, lanes] = jnp.concatenate(outs_i, axis=0)
        car_re[:, lanes] = jnp.broadcast_to(gr[ng - 1:ng], (SUBLANE, LANE))
        car_im[:, lanes] = jnp.broadcast_to(gi[ng - 1:ng], (SUBLANE, LANE))
        return carry

    lax.fori_loop(0, S5_LANES // LANE, lane_block, 0)

    sre_ref[0] = xre_ref[lt - SUBLANE:lt, :]
    sim_ref[0] = xim_ref[lt - SUBLANE:lt, :]

    y = (_dot(xre_ref[...].astype(BF16), cre_ref[...]) + _dot(xim_ref[...].astype(BF16), cim_ref[...])
         + d_ref[...] * u)
    y = _gelu_tanh(y)
    o_ref[...] = y * jax.nn.sigmoid(_dot(y.astype(BF16), wg_ref[...]))


def _s5_prompt(u, bre, bim, cre, cim, pre, pim, p8re, p8im, dvec, wglu, *, nb, seq, lt):
    m = u.shape[0]
    nt = seq // lt
    assert lt % (SUBLANE * SUBLANE) == 0 and (lt & (lt - 1)) == 0
    return pl.pallas_call(
        functools.partial(_s5_prompt_body, lt=lt),
        out_shape=[
            jax.ShapeDtypeStruct((m, GROUP_WIDTH), F32),
            jax.ShapeDtypeStruct((nb, SUBLANE, S5_LANES), F32),
            jax.ShapeDtypeStruct((nb, SUBLANE, S5_LANES), F32),
        ],
        grid=(nb, nt),
        in_specs=[
            pl.BlockSpec((lt, GROUP_WIDTH), lambda b, t: (b * nt + t, 0)),
            _const_spec(bre.shape), _const_spec(bim.shape),
            _const_spec(cre.shape), _const_spec(cim.shape),
            _const_spec(pre.shape), _const_spec(pim.shape),
            _const_spec(p8re.shape), _const_spec(p8im.shape),
            pl.BlockSpec((1, GROUP_WIDTH), lambda b, t: (0, 0)),
            _const_spec(wglu.shape),
        ],
        out_specs=[
            pl.BlockSpec((lt, GROUP_WIDTH), lambda b, t: (b * nt + t, 0)),
            pl.BlockSpec((1, SUBLANE, S5_LANES), lambda b, t: (b, 0, 0)),
            pl.BlockSpec((1, SUBLANE, S5_LANES), lambda b, t: (b, 0, 0)),
        ],
        scratch_shapes=[
            pltpu.VMEM((lt, S5_LANES), F32), pltpu.VMEM((lt, S5_LANES), F32),
            pltpu.VMEM((SUBLANE, S5_LANES), F32), pltpu.VMEM((SUBLANE, S5_LANES), F32),
            pltpu.VMEM((lt, LANE), F32), pltpu.VMEM((lt, LANE), F32),
        ],
        compiler_params=_cparams(("parallel", "arbitrary")),
        name="s5_prompt",
    )(u, bre, bim, cre, cim, pre, pim, p8re, p8im, dvec, wglu)


def _group_out(x, outs, gains_ref, wout_ref):
    y = x
    for g, o in enumerate(outs):
        n = _rms(o, gains_ref[g:g + 1, :]).astype(BF16)
        y = y + _dot(n, wout_ref[g * GROUP_WIDTH:(g + 1) * GROUP_WIDTH, :])
    return y


def _postmix_prompt_body(x_ref, a_ref, cv_ref, cb_ref, cc_ref, hv_ref, hc_ref, c_ref, d_ref,
                         gains_ref, cw_ref, wout_ref, o_ref, cs_ref, *, tm, tiles_per_seq):
    i = pl.program_id(0)
    xc = cc_ref[...] * cv_ref[...]
    halo = hc_ref[...] * hv_ref[...]
    halo = jnp.where(i % tiles_per_seq == 0, 0.0, halo)
    rowid = lax.broadcasted_iota(jnp.int32, xc.shape, 0)
    h7 = halo[SUBLANE - 1:SUBLANE, :]
    h6 = halo[SUBLANE - 2:SUBLANE - 1, :]
    prev1 = jnp.where(rowid == 0, h7, pltpu.roll(xc, 1, axis=0))
    prev2 = jnp.where(rowid == 0, h6, jnp.where(rowid == 1, h7, pltpu.roll(xc, 2, axis=0)))
    conv = cw_ref[0:1, :] * prev2 + cw_ref[1:2, :] * prev1 + cw_ref[2:3, :] * xc
    b = cb_ref[...] * conv
    cs_ref[0] = xc[tm - SUBLANE:tm, :]
    o_ref[...] = _group_out(x_ref[...], [a_ref[...], b, c_ref[...], d_ref[...]], gains_ref, wout_ref)


def _postmix_prompt(x, a, cv, cb, cc, c, d, gains, cw, wout, l, *, seq, tm):
    m, dm = x.shape
    tiles_per_seq = seq // tm
    nb = m // seq
    row = lambda n: pl.BlockSpec((tm, n), lambda i: (i, 0))
    halo = pl.BlockSpec((SUBLANE, GROUP_WIDTH),
                        lambda i: (jnp.maximum(i * (tm // SUBLANE) - 1, 0), 0))
    return pl.pallas_call(
        functools.partial(_postmix_prompt_body, tm=tm, tiles_per_seq=tiles_per_seq),
        out_shape=[jax.ShapeDtypeStruct((m, dm), F32),
                   jax.ShapeDtypeStruct((nb, SUBLANE, GROUP_WIDTH), F32)],
        grid=(m // tm,),
        in_specs=[row(dm), row(GROUP_WIDTH), row(GROUP_WIDTH), row(GROUP_WIDTH), row(GROUP_WIDTH),
                  halo, halo, row(GROUP_WIDTH), row(GROUP_WIDTH),
                  pl.BlockSpec(gains.shape, lambda i: (0, 0)),
                  pl.BlockSpec(cw.shape, lambda i: (0, 0)),
                  pl.BlockSpec((None,) + wout.shape[1:], lambda i: (l, 0, 0),
                               pipeline_mode=pl.Buffered(1))],
        out_specs=[row(dm),
                   pl.BlockSpec((1, SUBLANE, GROUP_WIDTH), lambda i: (i // tiles_per_seq, 0, 0))],
        compiler_params=_cparams(("arbitrary",)),
        name="postmix_prompt",
    )(x, a, cv, cb, cc, cv, cc, c, d, gains, cw, wout)


def _split3(x):
    hi = x.astype(BF16)
    r1 = x - hi.astype(F32)
    mid = r1.astype(BF16)
    lo = (r1 - mid.astype(F32)).astype(BF16)
    return hi, mid, lo


def _postmix_sample_body(x_ref, a_ref, cv_ref, cb_ref, cc_ref, s0_ref, s1_ref, olat_ref, u_ref,
                         x0re_ref, x0im_ref, gains_ref, cw_ref, wout_ref, wuv_ref,
                         bre_hi_ref, bre_lo_ref, bim_hi_ref, bim_lo_ref, cre_ref, cim_ref,
                         lre_ref, lim_ref, d_ref, wg_ref,
                         o_ref, xc_ref, sre_ref, sim_ref):
    xc = cc_ref[...] * cv_ref[...]
    conv = cw_ref[0:1, :] * s0_ref[...] + cw_ref[1:2, :] * s1_ref[...] + cw_ref[2:3, :] * xc
    b = cb_ref[...] * conv
    xc_ref[...] = xc
    cs = []
    for h in range(MLA_HEADS):
        ol = olat_ref[:, h * MLA_KV_RANK:(h + 1) * MLA_KV_RANK].astype(BF16)
        cs.append(_dot(ol, wuv_ref[h]))
    c = jnp.concatenate(cs, axis=1)
    u = u_ref[...]
    u_hi, u_mid, _ = _split3(u)
    bu_re = (_dot(u_hi, bre_hi_ref[...]) + _dot(u_mid, bre_hi_ref[...]) + _dot(u_hi, bre_lo_ref[...]))
    bu_im = (_dot(u_hi, bim_hi_ref[...]) + _dot(u_mid, bim_hi_ref[...]) + _dot(u_hi, bim_lo_ref[...]))
    lre = lre_ref[...]
    lim = lim_ref[...]
    x0re = x0re_ref[...]
    x0im = x0im_ref[...]
    nre = lre * x0re - lim * x0im + bu_re
    nim = lre * x0im + lim * x0re + bu_im
    sre_ref[...] = nre
    sim_ref[...] = nim
    y = _dot(nre.astype(BF16), cre_ref[...]) + _dot(nim.astype(BF16), cim_ref[...]) + d_ref[...] * u
    y = _gelu_tanh(y)
    d = y * jax.nn.sigmoid(_dot(y.astype(BF16), wg_ref[...]))
    o_ref[...] = _group_out(x_ref[...], [a_ref[...], b, c, d], gains_ref, wout_ref)


def _postmix_sample(x, a, cv, cb, cc, s0, s1, olat, u, x0re, x0im, gains, cw, wout, wuv,
                    bre_hi, bre_lo, bim_hi, bim_lo, cre, cim, lre, lim, dvec, wglu, l):
    m, dm = x.shape
    args = (x, a, cv, cb, cc, s0, s1, olat, u, x0re, x0im, gains, cw, wout, wuv,
            bre_hi, bre_lo, bim_hi, bim_lo, cre, cim, lre, lim, dvec, wglu)
    whole = lambda t: pl.BlockSpec(t.shape, lambda i: (0,) * t.ndim, pipeline_mode=pl.Buffered(1))
    in_specs = [whole(t) for t in args]
    in_specs[13] = pl.BlockSpec((None,) + wout.shape[1:], lambda i: (l, 0, 0),
                                pipeline_mode=pl.Buffered(1))
    out_shape = [jax.ShapeDtypeStruct((m, dm), F32),
                 jax.ShapeDtypeStruct((m, GROUP_WIDTH), F32),
                 jax.ShapeDtypeStruct((m, S5_LANES), F32),
                 jax.ShapeDtypeStruct((m, S5_LANES), F32)]
    return pl.pallas_call(
        _postmix_sample_body,
        out_shape=out_shape,
        grid=(1,),
        in_specs=in_specs,
        out_specs=[pl.BlockSpec(s.shape, lambda i: (0, 0)) for s in out_shape],
        compiler_params=_cparams(("arbitrary",)),
        name="postmix_sample",
    )(*args)


DECODE_AHEAD = 2
DECODE_SLOTS = DECODE_AHEAD + 1


def _ring_prefetch(b, nb, fetch):
    @pl.when(b == 0)
    def _():
        for a in range(DECODE_AHEAD):
            pl.when(a < nb)(functools.partial(fetch, a, a))

    @pl.when(b + DECODE_AHEAD < nb)
    def _():
        fetch(b + DECODE_AHEAD, (b + DECODE_AHEAD) % DECODE_SLOTS)


def _wait_all(buf_view, sem):
    pltpu.make_async_copy(buf_view, buf_view, sem).wait()


def _fox_decode_body(pt_ref, q_ref, kn_ref, vn_ref, lfn_ref, gpre_ref, tpre_ref,
                     k_hbm, v_hbm, lf_hbm, o_ref, kbuf, vbuf, lfbuf, sems, *, npages, page):
    b = pl.program_id(0)
    nb = pl.num_programs(0)
    slot = b % DECODE_SLOTS

    def fetch(seq, sl):
        def body(pg, carry):
            pid = pt_ref[seq, pg]
            row = pl.multiple_of(pg * page, page)
            hrow = pl.multiple_of(pg * HEAD_ROWS, HEAD_ROWS)
            pltpu.make_async_copy(k_hbm.at[pid], kbuf.at[sl, pl.ds(row, page), :], sems.at[sl, 0]).start()
            pltpu.make_async_copy(v_hbm.at[pid], vbuf.at[sl, pl.ds(row, page), :], sems.at[sl, 1]).start()
            pltpu.make_async_copy(lf_hbm.at[pid], lfbuf.at[sl, pl.ds(hrow, FOX_HEADS), :],
                                  sems.at[sl, 2]).start()
            return carry
        lax.fori_loop(0, npages, body, 0)

    @pl.when(b == 0)
    def _():
        lfbuf[...] = jnp.zeros(lfbuf.shape, F32)

    _ring_prefetch(b, nb, fetch)
    _wait_all(kbuf.at[slot], sems.at[slot, 0])
    _wait_all(vbuf.at[slot], sems.at[slot, 1])
    _wait_all(lfbuf.at[slot, pl.ds(0, npages * FOX_HEADS), :], sems.at[slot, 2])

    rows = npages * HEAD_ROWS
    gpre = gpre_ref[...]
    wp = sum(_dot(part, gpre) for part in _split3(lfbuf[slot]))
    tot = jnp.broadcast_to(wp[:, page - 1:page], (rows, page))
    tpre = tpre_ref[...]
    off = sum(_dot(tpre, part) for part in _split3(tot))
    cum = wp + off
    total = cum[rows - HEAD_ROWS:rows, page - 1:page]
    bias0 = lfn_ref[0] + total

    q8 = q_ref[0]
    qb = q8.astype(BF16)
    kb = kbuf[slot].astype(BF16)
    s_all = _dot_nt(qb, kb) * FOX_SCALE
    s_new = jnp.sum(q8 * kn_ref[0], axis=-1, keepdims=True) * FOX_SCALE
    zs = []
    mx = jnp.full((HEAD_ROWS, page), -jnp.inf, F32)
    for pg in range(npages):
        z = s_all[:, pg * page:(pg + 1) * page] + (bias0 - cum[pg * HEAD_ROWS:(pg + 1) * HEAD_ROWS, :])
        zs.append(z)
        mx = jnp.maximum(mx, z)
    m = jnp.maximum(jnp.max(mx, axis=-1, keepdims=True), s_new)
    ps = [jnp.exp(z - m) for z in zs]
    lsum = ps[0]
    for p in ps[1:]:
        lsum = lsum + p
    p_new = jnp.exp(s_new - m)
    l = jnp.sum(lsum, axis=-1, keepdims=True) + p_new
    pcat = jnp.concatenate([p.astype(BF16) for p in ps], axis=1)
    o = _dot(pcat, vbuf[slot].astype(BF16)) + p_new * vn_ref[0]
    o_ref[0] = o / l


def _fox_decode(pt, q8, kn, vn, lfn, gpre, tpre, kc, vc, lfc, *, npages, page):
    nseq = q8.shape[0]
    hd = FOX_HEAD_DIM
    blk = lambda r, n: pl.BlockSpec((1, r, n), lambda b, pt_ref: (b, 0, 0))
    const = lambda a: pl.BlockSpec(a.shape, lambda b, pt_ref: (0,) * a.ndim)
    grid_spec = pltpu.PrefetchScalarGridSpec(
        num_scalar_prefetch=1,
        grid=(nseq,),
        in_specs=[blk(HEAD_ROWS, hd), blk(1, hd), blk(1, hd), blk(HEAD_ROWS, page),
                  const(gpre), const(tpre),
                  pl.BlockSpec(memory_space=pl.ANY), pl.BlockSpec(memory_space=pl.ANY),
                  pl.BlockSpec(memory_space=pl.ANY)],
        out_specs=blk(HEAD_ROWS, hd),
        scratch_shapes=[
            pltpu.VMEM((DECODE_SLOTS, npages * page, hd), F32),
            pltpu.VMEM((DECODE_SLOTS, npages * page, hd), F32),
            pltpu.VMEM((DECODE_SLOTS, npages * HEAD_ROWS, page), F32),
            pltpu.SemaphoreType.DMA((DECODE_SLOTS, 3)),
        ],
    )
    return pl.pallas_call(
        functools.partial(_fox_decode_body, npages=npages, page=page),
        out_shape=jax.ShapeDtypeStruct((nseq, HEAD_ROWS, hd), F32),
        grid_spec=grid_spec,
        compiler_params=_cparams(("arbitrary",)),
        name="fox_decode",
    )(pt, q8, kn, vn, lfn, gpre, tpre, kc, vc, lfc)


def _mla_decode_body(pt_ref, ql_ref, qr_ref, cn_ref, rn_ref, c_hbm, r_hbm, o_ref,
                     cbuf, rbuf, sems, *, npages, page):
    b = pl.program_id(0)
    nb = pl.num_programs(0)
    slot = b % DECODE_SLOTS

    def fetch(seq, sl):
        def body(pg, carry):
            pid = pt_ref[seq, pg]
            row = pl.multiple_of(pg * page, page)
            pltpu.make_async_copy(c_hbm.at[pid], cbuf.at[sl, pl.ds(row, page), :], sems.at[sl, 0]).start()
            pltpu.make_async_copy(r_hbm.at[pid], rbuf.at[sl, :, pl.ds(row, page)], sems.at[sl, 1]).start()
            return carry
        lax.fori_loop(0, npages, body, 0)

    _ring_prefetch(b, nb, fetch)
    _wait_all(cbuf.at[slot], sems.at[slot, 0])
    _wait_all(rbuf.at[slot], sems.at[slot, 1])

    ql = ql_ref[0]
    qr = qr_ref[0]
    cb = cbuf[slot].astype(BF16)
    rb = rbuf[slot].astype(BF16)
    s = (_dot_nt(ql.astype(BF16), cb) + _dot(qr.astype(BF16), rb)) * MLA_SCALE
    s_new = (jnp.sum(ql * cn_ref[0], axis=-1, keepdims=True)
             + jnp.sum(qr * rn_ref[0], axis=-1, keepdims=True)) * MLA_SCALE
    m = jnp.maximum(jnp.max(s, axis=-1, keepdims=True), s_new)
    p = jnp.exp(s - m)
    p_new = jnp.exp(s_new - m)
    l = jnp.sum(p, axis=-1, keepdims=True) + p_new
    o = _dot(p.astype(BF16), cb) + p_new * cn_ref[0]
    o_ref[0] = o / l


def _mla_decode(pt, ql8, qr8, cn, rn, cc, rc, *, npages, page):
    nseq = ql8.shape[0]
    blk = lambda r, n: pl.BlockSpec((1, r, n), lambda b, pt_ref: (b, 0, 0))
    grid_spec = pltpu.PrefetchScalarGridSpec(
        num_scalar_prefetch=1,
        grid=(nseq,),
        in_specs=[blk(HEAD_ROWS, MLA_KV_RANK), blk(HEAD_ROWS, MLA_ROPE), blk(1, MLA_KV_RANK),
                  blk(1, MLA_ROPE),
                  pl.BlockSpec(memory_space=pl.ANY), pl.BlockSpec(memory_space=pl.ANY)],
        out_specs=blk(HEAD_ROWS, MLA_KV_RANK),
        scratch_shapes=[
            pltpu.VMEM((DECODE_SLOTS, npages * page, MLA_KV_RANK), F32),
            pltpu.VMEM((DECODE_SLOTS, MLA_ROPE, npages * page), F32),
            pltpu.SemaphoreType.DMA((DECODE_SLOTS, 2)),
        ],
    )
    return pl.pallas_call(
        functools.partial(_mla_decode_body, npages=npages, page=page),
        out_shape=jax.ShapeDtypeStruct((nseq, HEAD_ROWS, MLA_KV_RANK), F32),
        grid_spec=grid_spec,
        compiler_params=_cparams(("arbitrary",)),
        name="mla_decode",
    )(pt, ql8, qr8, cn, rn, cc, rc)


def _final_norm_body(x_ref, g_ref, o_ref):
    o_ref[...] = _rms(x_ref[...], g_ref[...])


def _final_norm(x, g, *, tm):
    m, d = x.shape
    return pl.pallas_call(
        _final_norm_body,
        out_shape=jax.ShapeDtypeStruct((m, d), F32),
        grid=(m // tm,),
        in_specs=[pl.BlockSpec((tm, d), lambda i: (i, 0)), pl.BlockSpec((1, d), lambda i: (0, 0))],
        out_specs=pl.BlockSpec((tm, d), lambda i: (i, 0)),
        compiler_params=_cparams(("parallel",)),
        name="final_norm",
    )(x, g)


def _rot_cols(w):
    shp = w.shape
    w4 = w.reshape(shp[:-1] + (shp[-1] // MLA_ROPE, 2, MLA_ROPE // 2))
    return w4[..., ::-1, :].reshape(shp)


def _prep_stacked(p):
    s = {}
    for k in ("1", "2"):
        s["g" + k] = p[f"w_ffn{k}_gate"].astype(BF16)
        s["u" + k] = p[f"w_ffn{k}_up"].astype(BF16)
        s["d" + k] = p[f"w_ffn{k}_down"].astype(BF16)
    wt = jnp.transpose(p["w_in"], (0, 2, 1)).astype(BF16)
    n_a = GROUP_WIDTH + 2 * FOX_HEAD_DIM
    s["w_in_a"] = wt[:, :n_a, :]
    s["w_in_b"] = wt[:, n_a + FOX_HEADS:, :]
    ff = wt[:, n_a:n_a + FOX_HEADS, :]
    k0 = n_a + FOX_HEADS + _SEG["kr"][0]
    half = MLA_ROPE // 2
    zeros = jnp.zeros((wt.shape[0], LANE - MLA_ROPE - FOX_HEADS, wt.shape[2]), BF16)
    s["w_in_c"] = jnp.concatenate(
        [wt[:, k0 + half:k0 + MLA_ROPE, :], wt[:, k0:k0 + half, :], ff, zeros], axis=1)
    s["w_out"] = p["w_out"].astype(BF16)
    return s


def _prep_layer(l, p):
    w = {}
    for k in ("1", "2"):
        w["n" + k] = p[f"norm_ffn{k}"][l][None, :]
    w["norm_mix"] = p["norm_mix"][l][None, :]
    w["bias_f"] = jnp.pad(p["fox_b_f"][l], (FF_LANE0, LANE - FF_LANE0 - FOX_HEADS))[None, :]
    wuq = p["mla_w_uq"][l].reshape(MLA_Q_RANK, MLA_HEADS, MLA_NOPE + MLA_ROPE)
    w["wq_nope"] = wuq[:, :, :MLA_NOPE].reshape(MLA_Q_RANK, MLA_HEADS * MLA_NOPE).astype(BF16)
    wr = wuq[:, :, MLA_NOPE:].reshape(MLA_Q_RANK, MLA_HEADS * MLA_ROPE)
    w["wq_rope"] = wr.astype(BF16)
    w["wq_rope_rot"] = _rot_cols(wr).astype(BF16)
    w["w_uk"] = jnp.transpose(p["mla_w_uk"][l], (1, 2, 0)).astype(BF16)
    w["w_uv"] = jnp.transpose(p["mla_w_uv"][l], (1, 0, 2)).astype(BF16)
    w["g_q"] = p["mla_norm_q"][l][None, :]
    w["g_kv"] = p["mla_norm_kv"][l][None, :]
    lr, li = p["s5_lambda_re"][l], p["s5_lambda_im"][l]
    step = jnp.exp(p["s5_log_step"][l])[:, None]
    mag = jnp.exp(lr * step)
    lbr, lbi = mag * jnp.cos(li * step), mag * jnp.sin(li * step)
    den = lr * lr + li * li
    cr = ((lbr - 1.0) * lr + lbi * li) / den
    ci = (lbi * lr - (lbr - 1.0) * li) / den
    bbr = cr[..., None] * p["s5_b_re"][l] - ci[..., None] * p["s5_b_im"][l]
    bbi = cr[..., None] * p["s5_b_im"][l] + ci[..., None] * p["s5_b_re"][l]
    eye = jnp.eye(S5_GROUPS, dtype=F32)
    bmat = lambda t: jnp.einsum("gnc,gh->gchn", t, eye).reshape(GROUP_WIDTH, S5_LANES)
    cmat = lambda t: jnp.einsum("gcn,gh->gnhc", t, eye).reshape(S5_LANES, GROUP_WIDTH)
    bre, bim = bmat(bbr), bmat(bbi)
    w["bre"], w["bim"] = bre.astype(BF16), bim.astype(BF16)
    w["bre_lo"] = (bre - w["bre"].astype(F32)).astype(BF16)
    w["bim_lo"] = (bim - w["bim"].astype(F32)).astype(BF16)
    w["cre"] = cmat(p["s5_c_re"][l]).astype(BF16)
    w["cim"] = cmat(-p["s5_c_im"][l]).astype(BF16)
    pr, pi = [lbr.reshape(1, S5_LANES)], [lbi.reshape(1, S5_LANES)]
    for _ in range(15):
        r, i = pr[-1], pi[-1]
        pr.append(r * r - i * i)
        pi.append(2.0 * r * i)
    w["pow_re"], w["pow_im"] = jnp.concatenate(pr, axis=0), jnp.concatenate(pi, axis=0)
    qr, qi = [pr[0]], [pi[0]]
    for _ in range(SUBLANE - 1):
        r, i = qr[-1], qi[-1]
        qr.append(r * pr[0] - i * pi[0])
        qi.append(r * pi[0] + i * pr[0])
    w["p8_re"], w["p8_im"] = jnp.concatenate(qr, axis=0), jnp.concatenate(qi, axis=0)
    w["s5_d"] = p["s5_d"][l][None, :]
    w["w_glu"] = p["s5_w_glu"][l].astype(BF16)
    w["gains"] = p["norm_group_out"][l].reshape(4, GROUP_WIDTH)
    w["conv_w"] = jnp.pad(p["conv_w"][l], ((0, SUBLANE - CONV_WIDTH), (0, 0)))
    return w


def _rope_tables(pos):
    half = MLA_ROPE // 2
    inv_freq = ROPE_THETA ** (-jnp.arange(half, dtype=F32) / half)
    ang = pos.astype(F32)[:, None] * inv_freq
    cos, sin = jnp.cos(ang), jnp.sin(ang)
    cos64 = jnp.concatenate([cos, cos], axis=1)
    sin64 = jnp.concatenate([-sin, sin], axis=1)
    return jnp.tile(cos64, (1, MLA_HEADS)), jnp.tile(sin64, (1, MLA_HEADS))


def _decode_consts(npages, page):
    r = np.arange(npages * HEAD_ROWS)
    c = np.arange(page)
    gpre = (c[:, None] <= c[None, :]).astype(np.float32)
    tpre = (((r % HEAD_ROWS)[:, None] == (r % HEAD_ROWS)[None, :])
            & ((r // HEAD_ROWS)[None, :] < (r // HEAD_ROWS)[:, None])).astype(np.float32)
    return jnp.asarray(gpre, BF16), jnp.asarray(tpre, BF16)


def _pick(m, pref):
    for t in pref:
        if m % t == 0:
            return t
    return m


def kernel(x_prompt, x_sample, cache_fox_k, cache_fox_v, cache_fox_logf, cache_mla_ckv, cache_mla_krope, state_conv, state_s5_re, state_s5_im, page_table, norm_ffn1, w_ffn1_gate, w_ffn1_up, w_ffn1_down, norm_mix, w_in, fox_b_f, conv_w, mla_norm_q, mla_w_uq, mla_norm_kv, mla_w_uk, mla_w_uv, s5_lambda_re, s5_lambda_im, s5_log_step, s5_b_re, s5_b_im, s5_c_re, s5_c_im, s5_d, s5_w_glu, norm_group_out, w_out, norm_ffn2, w_ffn2_gate, w_ffn2_up, w_ffn2_down, norm_final):
    p = dict(norm_ffn1=norm_ffn1, w_ffn1_gate=w_ffn1_gate, w_ffn1_up=w_ffn1_up,
             w_ffn1_down=w_ffn1_down, norm_mix=norm_mix, w_in=w_in, fox_b_f=fox_b_f, conv_w=conv_w,
             mla_norm_q=mla_norm_q, mla_w_uq=mla_w_uq, mla_norm_kv=mla_norm_kv, mla_w_uk=mla_w_uk,
             mla_w_uv=mla_w_uv, s5_lambda_re=s5_lambda_re, s5_lambda_im=s5_lambda_im,
             s5_log_step=s5_log_step, s5_b_re=s5_b_re, s5_b_im=s5_b_im, s5_c_re=s5_c_re,
             s5_c_im=s5_c_im, s5_d=s5_d, s5_w_glu=s5_w_glu, norm_group_out=norm_group_out,
             w_out=w_out, norm_ffn2=norm_ffn2, w_ffn2_gate=w_ffn2_gate, w_ffn2_up=w_ffn2_up,
             w_ffn2_down=w_ffn2_down)
    depth = w_in.shape[0]
    bp, sp, dm = x_prompt.shape
    bd, sd, _ = x_sample.shape
    assert sd == 1, "sample group is a single-token decode step"
    n_pool, page = cache_fox_k.shape[1], cache_fox_k.shape[2]
    npages = page_table.shape[1]
    n_past = npages * page
    mp = bp * sp
    tf = 1024

    tm_p = _pick(mp, (512, 256, 128))
    tm_d = bd
    tq = _pick(sp, (512, 256, 128))
    lt = _pick(sp, (256, 128))

    cos_p, sin_p = _rope_tables(jnp.arange(sp))
    cos_d, sin_d = _rope_tables(jnp.full((bd,), n_past))
    gpre, tpre = _decode_consts(npages, page)

    kc = cache_fox_k.reshape(depth * n_pool, page, FOX_HEAD_DIM)
    vc = cache_fox_v.reshape(depth * n_pool, page, FOX_HEAD_DIM)
    lfc = cache_fox_logf.transpose(0, 1, 3, 2).reshape(depth * n_pool, FOX_HEADS, page)
    ckc = cache_mla_ckv.reshape(depth * n_pool, page, MLA_KV_RANK)
    krc = cache_mla_krope.transpose(0, 1, 3, 2).reshape(depth * n_pool, MLA_ROPE, page)

    xp = x_prompt.reshape(mp, dm)
    xd = x_sample.reshape(bd, dm)
    acc = {k: [] for k in ("fox_k_p", "fox_v_p", "fox_logf_p", "mla_ckv_p", "mla_krope_p", "conv_p",
                           "s5_re_p", "s5_im_p", "fox_k_s", "fox_v_s", "fox_logf_s", "mla_ckv_s",
                           "mla_krope_s", "conv_s", "s5_re_s", "s5_im_s")}
    pad_heads = lambda t: jnp.pad(t, ((0, 0), (0, HEAD_ROWS - t.shape[1]), (0, 0)))

    ws = _prep_stacked(p)
    for l in range(depth):
        w = _prep_layer(l, p)
        xp = _ffn(xp, w["n1"], ws["g1"], ws["u1"], ws["d1"], l, tm=tm_p, tf=tf)
        fq, fk, fv, cv, cb, cc, mq, mkv, su, kr, krr, lf = _inproj(
            xp, w["norm_mix"], ws["w_in_a"], ws["w_in_b"], ws["w_in_c"], w["bias_f"], l,
            tm=_pick(mp, (256, 128)))
        logf = lf[:, FF_LANE0:FF_LANE0 + FOX_HEADS]
        cum = _cumsum_lanes(logf.reshape(bp, sp, FOX_HEADS).transpose(0, 2, 1).reshape(bp * FOX_HEADS, sp))
        ck = cum.reshape(bp, FOX_HEADS, sp)
        cq = jnp.broadcast_to(ck.transpose(0, 2, 1)[..., None], (bp, sp, FOX_HEADS, LANE))
        cq = cq.reshape(mp, FOX_HEADS * LANE)
        a_p = _fox_prompt(fq, fk, fv, cq, ck, nb=bp, seq=sp, tq=tq, tk=tq)
        ql, qr, ckv, ckvb, krope, kropeb = _mla_proj(
            mq, mkv, kr, krr, cos_p, sin_p, w["g_q"], w["g_kv"], w["wq_nope"], w["wq_rope"],
            w["wq_rope_rot"], w["w_uk"], tm=_pick(sp, (256, 128)))
        c_p = _mla_prompt(ql, qr, ckvb, kropeb, w["w_uv"], nb=bp, seq=sp, tq=tq, tk=tq)
        d_p, sre_p, sim_p = _s5_prompt(su, w["bre"], w["bim"], w["cre"], w["cim"], w["pow_re"],
                                       w["pow_im"], w["p8_re"], w["p8_im"], w["s5_d"], w["w_glu"],
                                       nb=bp, seq=sp, lt=lt)
        xp, cs_p = _postmix_prompt(xp, a_p, cv, cb, cc, c_p, d_p, w["gains"], w["conv_w"], ws["w_out"], l,
                                   seq=sp, tm=_pick(sp, (256, 128)))
        xp = _ffn(xp, w["n2"], ws["g2"], ws["u2"], ws["d2"], l, tm=tm_p, tf=tf)
        acc["fox_k_p"].append(fk.reshape(bp, sp, 1, FOX_HEAD_DIM))
        acc["fox_v_p"].append(fv.reshape(bp, sp, 1, FOX_HEAD_DIM))
        acc["fox_logf_p"].append(logf.reshape(bp, sp, FOX_HEADS))
        acc["mla_ckv_p"].append(ckv.reshape(bp, sp, MLA_KV_RANK))
        acc["mla_krope_p"].append(krope.reshape(bp, sp, MLA_ROPE))
        acc["conv_p"].append(cs_p[:, SUBLANE - (CONV_WIDTH - 1):, :])
        acc["s5_re_p"].append(sre_p[:, SUBLANE - 1, :].reshape(bp, S5_GROUPS, S5_STATE))
        acc["s5_im_p"].append(sim_p[:, SUBLANE - 1, :].reshape(bp, S5_GROUPS, S5_STATE))

        xd = _ffn(xd, w["n1"], ws["g1"], ws["u1"], ws["d1"], l, tm=tm_d, tf=tf)
        fq, fk, fv, cv, cb, cc, mq, mkv, su, kr, krr, lf = _inproj(
            xd, w["norm_mix"], ws["w_in_a"], ws["w_in_b"], ws["w_in_c"], w["bias_f"], l, tm=tm_d)
        logf = lf[:, FF_LANE0:FF_LANE0 + FOX_HEADS]
        pt = page_table + l * n_pool
        q8 = pad_heads(fq.reshape(bd, FOX_HEADS, FOX_HEAD_DIM))
        lfn = jnp.broadcast_to(pad_heads(logf[:, :, None]), (bd, HEAD_ROWS, page))
        a8 = _fox_decode(pt, q8, fk[:, None, :], fv[:, None, :], lfn, gpre, tpre, kc, vc, lfc,
                         npages=npages, page=page)
        a_d = a8[:, :FOX_HEADS, :].reshape(bd, GROUP_WIDTH)
        ql, qr, ckv, ckvb, krope, kropeb = _mla_proj(
            mq, mkv, kr, krr, cos_d, sin_d, w["g_q"], w["g_kv"], w["wq_nope"], w["wq_rope"],
            w["wq_rope_rot"], w["w_uk"], tm=tm_d)
        ql8 = pad_heads(ql.astype(F32).reshape(bd, MLA_HEADS, MLA_KV_RANK))
        qr8 = pad_heads(qr.astype(F32).reshape(bd, MLA_HEADS, MLA_ROPE))
        ol8 = _mla_decode(pt, ql8, qr8, ckv[:, None, :], krope[:, None, :], ckc, krc,
                          npages=npages, page=page)
        olat = ol8[:, :MLA_HEADS, :].reshape(bd, MLA_HEADS * MLA_KV_RANK)
        sc = state_conv[l]
        xd, xc_d, sre_d, sim_d = _postmix_sample(
            xd, a_d, cv, cb, cc, sc[:, 0, :], sc[:, 1, :], olat, su,
            state_s5_re[l].reshape(bd, S5_LANES), state_s5_im[l].reshape(bd, S5_LANES),
            w["gains"], w["conv_w"], ws["w_out"], w["w_uv"], w["bre"], w["bre_lo"], w["bim"], w["bim_lo"],
            w["cre"], w["cim"], w["pow_re"][0:1], w["pow_im"][0:1], w["s5_d"], w["w_glu"], l)
        xd = _ffn(xd, w["n2"], ws["g2"], ws["u2"], ws["d2"], l, tm=tm_d, tf=tf)
        acc["fox_k_s"].append(fk.reshape(bd, 1, 1, FOX_HEAD_DIM))
        acc["fox_v_s"].append(fv.reshape(bd, 1, 1, FOX_HEAD_DIM))
        acc["fox_logf_s"].append(logf.reshape(bd, 1, FOX_HEADS))
        acc["mla_ckv_s"].append(ckv.reshape(bd, 1, MLA_KV_RANK))
        acc["mla_krope_s"].append(krope.reshape(bd, 1, MLA_ROPE))
        acc["conv_s"].append(jnp.stack([sc[:, 1, :], xc_d], axis=1))
        acc["s5_re_s"].append(sre_d.reshape(bd, S5_GROUPS, S5_STATE))
        acc["s5_im_s"].append(sim_d.reshape(bd, S5_GROUPS, S5_STATE))

    gfin = norm_final[None, :]
    y_prompt = _final_norm(xp, gfin, tm=tm_p).reshape(bp, sp, dm)
    y_sample = _final_norm(xd, gfin, tm=tm_d).reshape(bd, sd, dm)
    st = {k: jnp.stack(v) for k, v in acc.items()}
    return (y_prompt, y_sample,
            st["fox_k_p"], st["fox_v_p"], st["fox_logf_p"], st["mla_ckv_p"], st["mla_krope_p"],
            st["conv_p"], st["s5_re_p"], st["s5_im_p"],
            st["fox_k_s"], st["fox_v_s"], st["fox_logf_s"], st["mla_ckv_s"], st["mla_krope_s"],
            st["conv_s"], st["s5_re_s"], st["s5_im_s"])
```
